```python
import math
import jax, jax.numpy as jnp
from jax import lax
import numpy as np

D_MODEL = 2048
BATCH = 4
SEQ = 2048
DEPTH = 2
DEC_BATCH = 8
DEC_SEQ = 4096
PAST_LEN = 128

PLE_DIM = 256
EPS = 1e-6
CONV_CH = 512
CONV_WIDTH = 31
CONV_PAD = (CONV_WIDTH - 1) // 2
SSD_HEAD_DIM = 64
SSD_HEADS = 12
SSD_WIDTH = SSD_HEADS * SSD_HEAD_DIM
SSD_GROUPS = 2
SSD_HPG = SSD_HEADS // SSD_GROUPS
SSD_STATE = 128
SSD_CONV_WIDTH = 5
SSD_CONV_PAD = (SSD_CONV_WIDTH - 1) // 2
SSD_CHUNK = 128
SSD_XBC = SSD_WIDTH + 2 * SSD_GROUPS * SSD_STATE
ATT_HEADS = 6
ATT_HEAD_DIM = 64
ATT_WIDTH = ATT_HEADS * 2 * ATT_HEAD_DIM
Q_BLOCK = 128
REL_BUCKETS = 32
REL_MAX_DIST = 128
MIX_WIDTH = CONV_CH + SSD_WIDTH + ATT_WIDTH
D_FF = 4 * D_MODEL
IN_SIZES = (CONV_CH, CONV_CH, SSD_WIDTH, SSD_XBC, 2 * SSD_HEADS, ATT_WIDTH, ATT_WIDTH, ATT_WIDTH)
IN_COLS = sum(IN_SIZES)
IN_SPLITS = tuple(int(v) for v in np.cumsum(IN_SIZES)[:-1])

kernel_name = "hybrid_bidir_conv_ssd_diffattn_encoder"


def rmsnorm(x, g):
    xf = x.astype(jnp.float32)
    y = xf * lax.rsqrt(jnp.mean(xf * xf, axis=-1, keepdims=True) + EPS)
    return (y * g.astype(jnp.float32)).astype(x.dtype)


def layernorm(x, g, b):
    xf = x.astype(jnp.float32)
    mu = jnp.mean(xf, axis=-1, keepdims=True)
    var = jnp.mean(jnp.square(xf - mu), axis=-1, keepdims=True)
    y = (xf - mu) * lax.rsqrt(var + EPS)
    return (y * g.astype(jnp.float32) + b.astype(jnp.float32)).astype(x.dtype)


def depthwise_conv(x, w, pad):
    c = x.shape[-1]
    return lax.conv_general_dilated(
        x, w[:, None, :].astype(x.dtype), window_strides=(1,), padding=[(pad, pad)],
        dimension_numbers=("NWC", "WIO", "NWC"), feature_group_count=c)


def rel_bucket(rel):
    nb = REL_BUCKETS // 2
    max_exact = nb // 2
    ret = jnp.where(rel > 0, nb, 0)
    n = jnp.abs(rel)
    nf = jnp.maximum(n, 1).astype(jnp.float32)
    large = max_exact + (jnp.log(nf / max_exact) / math.log(REL_MAX_DIST / max_exact)
                         * (nb - max_exact)).astype(jnp.int32)
    large = jnp.minimum(large, nb - 1)
    return ret + jnp.where(n < max_exact, n, large)


def ssd_scan(xh, dt, A, B, C):
    b, l, G, R, P = xh.shape
    N = B.shape[-1]
    c = l // SSD_CHUNK
    Q = SSD_CHUNK
    dt_ = xh.dtype
    xh = xh.reshape(b, c, Q, G, R, P)
    dt = dt.reshape(b, c, Q, G, R)
    B = B.reshape(b, c, Q, G, N)
    C = C.reshape(b, c, Q, G, N)
    a_cs = jnp.cumsum((dt * A).astype(jnp.float32), axis=2)
    xdt = xh * dt[..., None]
    a_t = jnp.moveaxis(a_cs, 2, -1)
    seg = a_t[..., :, None] - a_t[..., None, :]
    tril = jnp.tril(jnp.ones((Q, Q), dtype=bool))
    L = jnp.exp(jnp.where(tril, seg, -jnp.inf)).astype(dt_)
    CB = jnp.einsum("bclgn,bcsgn->bcgls", C, B)
    y_diag = jnp.einsum("bcgls,bcgrls,bcsgrp->bclgrp", CB, L, xdt)
    decay_states = jnp.exp(a_cs[:, :, -1:] - a_cs).astype(dt_)
    states = jnp.einsum("bcsgn,bcsgr,bcsgrp->bcgrpn", B, decay_states, xdt)
    chunk_decay = jnp.exp(a_cs[:, :, -1]).astype(dt_)

    def step(carry, inp):
        st, dec = inp
        return carry * dec[..., None, None] + st, carry

    _, prev = lax.scan(step, jnp.zeros_like(states[:, 0]),
                       (jnp.moveaxis(states, 1, 0), jnp.moveaxis(chunk_decay, 1, 0)))
    prev = jnp.moveaxis(prev, 0, 1)
    y_off = jnp.einsum("bclgn,bcgrpn,bclgr->bclgrp", C, prev, jnp.exp(a_cs).astype(dt_))
    return (y_diag + y_off).reshape(b, l, G, R, P)


def conv_module(val, gate, conv_w, conv_b, norm_g, norm_b):
    u = val * jax.nn.sigmoid(gate)
    u = depthwise_conv(u, conv_w, CONV_PAD) + conv_b
    u = layernorm(u, norm_g, norm_b)
    return jax.nn.silu(u)


def ssd_mixer(z, xbc, dt_raw, conv_w, conv_b, dt_bias, a_log, d_skip, norm_g):
    b, l, _ = xbc.shape
    xbc = jax.nn.silu(depthwise_conv(xbc, conv_w, SSD_CONV_PAD) + conv_b)
    xs, Bm, Cm = jnp.split(xbc, [SSD_WIDTH, SSD_WIDTH + SSD_GROUPS * SSD_STATE], axis=-1)
    xh = xs.reshape(b, l, SSD_GROUPS, SSD_HPG, SSD_HEAD_DIM)
    Bm = Bm.reshape(b, l, SSD_GROUPS, SSD_STATE)
    Cm = Cm.reshape(b, l, SSD_GROUPS, SSD_STATE)
    dt = jax.nn.softplus(dt_raw.reshape(b, l, 2, SSD_GROUPS, SSD_HPG)
                         + dt_bias.reshape(2, SSD_GROUPS, SSD_HPG))
    A = -jnp.exp(a_log.reshape(2, SSD_GROUPS, SSD_HPG))
    y_f = ssd_scan(xh, dt[:, :, 0], A[0], Bm, Cm)
    fl = lambda t: jnp.flip(t, axis=1)
    y_b = fl(ssd_scan(fl(xh), fl(dt[:, :, 1]), A[1], fl(Bm), fl(Cm)))
    y = y_f + y_b + d_skip.reshape(SSD_GROUPS, SSD_HPG)[:, :, None] * xh
    y = y.reshape(b, l, SSD_WIDTH)
    return rmsnorm(y * jax.nn.silu(z), norm_g)


def diff_attention(q, k, v, lq1, lk1, lq2, lk2, subln_g, rel_bias, lam_init):
    b, s, _ = q.shape
    q = q.reshape(b, s, ATT_HEADS, 2, ATT_HEAD_DIM) * (ATT_HEAD_DIM ** -0.5)
    k = k.reshape(b, s, ATT_HEADS, 2, ATT_HEAD_DIM)
    v = v.reshape(b, s, ATT_HEADS, 2 * ATT_HEAD_DIM)
    f32 = jnp.float32
    lam = (jnp.exp(jnp.sum(lq1.astype(f32) * lk1.astype(f32)))
           - jnp.exp(jnp.sum(lq2.astype(f32) * lk2.astype(f32))) + lam_init)
    n_blk = s // Q_BLOCK
    q_blocks = jnp.moveaxis(q.reshape(b, n_blk, Q_BLOCK, ATT_HEADS, 2, ATT_HEAD_DIM), 1, 0)
    k_pos = jnp.arange(s, dtype=jnp.int32)

    def block(args):
        q_blk, start = args
        logits = jnp.einsum("bqhmd,bkhmd->bhmqk", q_blk, k).astype(f32)
        rel = k_pos[None, :] - (start + jnp.arange(Q_BLOCK, dtype=jnp.int32))[:, None]
        bias = jnp.transpose(rel_bias[rel_bucket(rel)], (2, 0, 1)).astype(f32)
        probs = jax.nn.softmax(logits + bias[None, :, None], axis=-1)
        a = (probs[:, :, 0] - lam * probs[:, :, 1]).astype(v.dtype)
        return jnp.einsum("bhqk,bkhe->bqhe", a, v)

    o = lax.map(block, (q_blocks, jnp.arange(n_blk, dtype=jnp.int32) * Q_BLOCK))
    o = jnp.moveaxis(o, 0, 1).reshape(b, s, ATT_HEADS, 2 * ATT_HEAD_DIM)
    o = rmsnorm(o, subln_g) * (1.0 - lam_init)
    return o.reshape(b, s, ATT_WIDTH)


def setup_inputs(seed: int = 0) -> dict:
    key = jax.random.key(seed)
    ks = jax.random.split(key, 32)
    nrm = lambda k, shape, scale: jax.random.normal(k, shape, jnp.float32) * scale
    gain = lambda k, shape: 1.0 + 0.02 * jax.random.normal(k, shape, jnp.float32)
    dt = jnp.exp(jax.random.uniform(ks[10], (DEPTH, 2, SSD_HEADS), jnp.float32,
                                    math.log(1e-3), math.log(1e-1)))
    dt_bias = dt + jnp.log(-jnp.expm1(-dt))
    a_log = jnp.log(jax.random.uniform(ks[11], (DEPTH, 2, SSD_HEADS), jnp.float32, 1.0, 16.0))
    return {
        "x_prompt": nrm(ks[0], (BATCH, SEQ, D_MODEL), 1.0),
        "x_sample": nrm(ks[1], (DEC_BATCH, DEC_SEQ, D_MODEL), 1.0),
        "p_prompt": nrm(ks[2], (DEPTH, BATCH, SEQ, PLE_DIM), 1.0),
        "p_sample": nrm(ks[3], (DEPTH, DEC_BATCH, DEC_SEQ, PLE_DIM), 1.0),
        "norm_mix_g": gain(ks[4], (DEPTH, D_MODEL)),
        "w_in": nrm(ks[5], (DEPTH, D_MODEL, IN_COLS), D_MODEL ** -0.5),
        "conv_w": nrm(ks[6], (DEPTH, CONV_WIDTH, CONV_CH), CONV_WIDTH ** -0.5),
        "conv_b": nrm(ks[7], (DEPTH, CONV_CH), 0.02),
        "conv_norm_g": gain(ks[8], (DEPTH, CONV_CH)),
        "conv_norm_b": nrm(ks[9], (DEPTH, CONV_CH), 0.02),
        "ssd_conv_w": nrm(ks[12], (DEPTH, SSD_CONV_WIDTH, SSD_XBC), SSD_CONV_WIDTH ** -0.5),
        "ssd_conv_b": nrm(ks[13], (DEPTH, SSD_XBC), 0.02),
        "ssd_dt_bias": dt_bias,
        "ssd_a_log": a_log,
        "ssd_d": gain(ks[14], (DEPTH, SSD_HEADS)),
        "ssd_norm_g": gain(ks[15], (DEPTH, SSD_WIDTH)),
        "lambda_q1": nrm(ks[16], (DEPTH, ATT_HEAD_DIM), 0.1),
        "lambda_k1": nrm(ks[17], (DEPTH, ATT_HEAD_DIM), 0.1),
        "lambda_q2": nrm(ks[18], (DEPTH, ATT_HEAD_DIM), 0.1),
        "lambda_k2": nrm(ks[19], (DEPTH, ATT_HEAD_DIM), 0.1),
        "attn_subln_g": gain(ks[20], (DEPTH, 2 * ATT_HEAD_DIM)),
        "rel_bias": nrm(ks[21], (REL_BUCKETS, ATT_HEADS), 0.5),
        "w_out": nrm(ks[22], (DEPTH, MIX_WIDTH, D_MODEL), MIX_WIDTH ** -0.5),
        "norm_mlp_g": gain(ks[23], (DEPTH, D_MODEL)),
        "w_up": nrm(ks[24], (DEPTH, D_MODEL, D_FF), D_MODEL ** -0.5),
        "w_down": nrm(ks[25], (DEPTH, D_FF, D_MODEL), D_FF ** -0.5),
        "norm_ple_g": gain(ks[26], (DEPTH, D_MODEL)),
        "w_ple": nrm(ks[27], (DEPTH, PLE_DIM, D_MODEL), PLE_DIM ** -0.5),
        "w_ple_gate": nrm(ks[28], (DEPTH, D_MODEL, D_MODEL), D_MODEL ** -0.5),
        "final_norm_g": gain(ks[29], (D_MODEL,)),
    }


def reference(x_prompt, x_sample, p_prompt, p_sample, norm_mix_g, w_in, conv_w, conv_b,
              conv_norm_g, conv_norm_b, ssd_conv_w, ssd_conv_b, ssd_dt_bias, ssd_a_log, ssd_d,
              ssd_norm_g, lambda_q1, lambda_k1, lambda_q2, lambda_k2, attn_subln_g, rel_bias,
              w_out, norm_mlp_g, w_up, w_down, norm_ple_g, w_ple, w_ple_gate, final_norm_g):

    def run(x, p):
        h = x
        for i in range(DEPTH):
            lam_init = 0.8 - 0.6 * math.exp(-0.3 * i)
            u = rmsnorm(h, norm_mix_g[i])
            proj = jnp.einsum("bsd,dc->bsc", u, w_in[i])
            c_val, c_gate, s_z, s_xbc, s_dt, a_q, a_k, a_v = jnp.split(proj, IN_SPLITS, axis=-1)
            y_conv = conv_module(c_val, c_gate, conv_w[i], conv_b[i], conv_norm_g[i], conv_norm_b[i])
            y_ssd = ssd_mixer(s_z, s_xbc, s_dt, ssd_conv_w[i], ssd_conv_b[i], ssd_dt_bias[i],
                              ssd_a_log[i], ssd_d[i], ssd_norm_g[i])
            y_att = diff_attention(a_q, a_k, a_v, lambda_q1[i], lambda_k1[i], lambda_q2[i],
                                   lambda_k2[i], attn_subln_g[i], rel_bias, lam_init)
            mix = jnp.concatenate([y_conv, y_ssd, y_att], axis=-1)
            h = h + jnp.einsum("bsc,cd->bsd", mix, w_out[i])
            u = rmsnorm(h, norm_mlp_g[i])
            hid = jnp.square(jax.nn.relu(jnp.einsum("bsd,df->bsf", u, w_up[i])))
            h = h + jnp.einsum("bsf,fd->bsd", hid, w_down[i])
            gate = jax.nn.sigmoid(jnp.einsum("bsd,de->bse", rmsnorm(h, norm_ple_g[i]), w_ple_gate[i]))
            h = h + jnp.einsum("bsk,kd->bsd", p[i], w_ple[i]) * gate
        return rmsnorm(h, final_norm_g)

    y_prompt = run(x_prompt, p_prompt)
    y_sample = run(x_sample, p_sample)
    return (y_prompt, y_sample)
```

```python
import functools
import math

import jax
import jax.numpy as jnp
import numpy as np
from jax import lax
from jax.experimental import pallas as pl
from jax.experimental.pallas import tpu as pltpu

F32 = jnp.float32
BF16 = jnp.bfloat16

D_MODEL = 2048
DEPTH = 2
PLE_DIM = 256
EPS = 1e-6
CONV_CH = 512
CONV_WIDTH = 31
CONV_PAD = (CONV_WIDTH - 1) // 2
SSD_HEAD_DIM = 64
SSD_HEADS = 12
SSD_WIDTH = SSD_HEADS * SSD_HEAD_DIM
SSD_GROUPS = 2
SSD_HPG = SSD_HEADS // SSD_GROUPS
SSD_STATE = 128
SSD_CONV_WIDTH = 5
SSD_CONV_PAD = (SSD_CONV_WIDTH - 1) // 2
SSD_CHUNK = 128
SSD_XBC = SSD_WIDTH + 2 * SSD_GROUPS * SSD_STATE
ATT_HEADS = 6
ATT_HEAD_DIM = 64
ATT_WIDTH = ATT_HEADS * 2 * ATT_HEAD_DIM
REL_BUCKETS = 32
REL_MAX_DIST = 128
D_FF = 4 * D_MODEL

LANES = 128
SUBLANES = 8
VMEM_LIMIT = 56 * 1024 * 1024

SSD_DT_COLS = LANES
SSD_SLAB = SSD_WIDTH + SSD_XBC + SSD_DT_COLS
CONV_SLAB = 2 * CONV_CH
ATT_SLAB = 3 * ATT_WIDTH
SSD_GW = SSD_HPG * SSD_HEAD_DIM

ATT_TILE = 512
N_BIAS_TILES = 5
CONV_TILE = 128
CONV_HALO = 16
CONV_ROWS = 32
SSD_HALO = SUBLANES
ROW_TILE = 512
FF_TILE = 1024


def _cparams(sem):
    return pltpu.CompilerParams(dimension_semantics=sem, vmem_limit_bytes=VMEM_LIMIT)


def _rms(x, g):
    ms = jnp.mean(x * x, axis=-1, keepdims=True)
    return (x * lax.rsqrt(ms + EPS)) * g


def _silu(x):
    return x * jax.nn.sigmoid(x)


def _dot(a, b):
    return jnp.dot(a, b, preferred_element_type=F32)


def _dot_exact(a, b):
    return jnp.dot(a, b, preferred_element_type=F32, precision=lax.Precision.HIGHEST)


def _dot_nt(a, b):
    return lax.dot_general(a, b, (((1,), (1,)), ((), ())), preferred_element_type=F32)


def _norm_proj_kernel(x_ref, g_ref, w_ref, o_ref):
    xn = _rms(x_ref[...], g_ref[...]).astype(BF16)
    o_ref[...] = _dot(xn, w_ref[...]).astype(o_ref.dtype)


def norm_proj(x, g, w, out_dtype, tm=ROW_TILE):
    m, d = x.shape
    n = w.shape[1]
    return pl.pallas_call(
        _norm_proj_kernel,
        grid=(m // tm,),
        in_specs=[pl.BlockSpec((tm, d), lambda i: (i, 0)),
                  pl.BlockSpec((1, d), lambda i: (0, 0)),
                  pl.BlockSpec((d, n), lambda i: (0, 0))],
        out_specs=pl.BlockSpec((tm, n), lambda i: (i, 0)),
        out_shape=jax.ShapeDtypeStruct((m, n), out_dtype),
        compiler_params=_cparams(("parallel",)),
        name="norm_proj",
    )(x, g, w)


def _conv_kernel(cur_ref, left_ref, right_ref, w_ref, b_ref, g_ref, beta_ref, o_ref, buf_ref):
    t = pl.program_id(1)
    nt = pl.num_programs(1)

    def glu(blk):
        return blk[:, :CONV_CH] * jax.nn.sigmoid(blk[:, CONV_CH:])

    buf_ref[0:CONV_HALO, :] = jnp.where(t > 0, glu(left_ref[0]), 0.0)
    buf_ref[CONV_HALO:CONV_HALO + CONV_TILE, :] = glu(cur_ref[0])
    buf_ref[CONV_HALO + CONV_TILE:, :] = jnp.where(t < nt - 1, glu(right_ref[0]), 0.0)

    w = w_ref[...]
    for r0 in range(0, CONV_TILE, CONV_ROWS):
        acc = jnp.broadcast_to(b_ref[...], (CONV_ROWS, CONV_CH))
        for k in range(CONV_WIDTH):
            start = r0 + CONV_HALO - CONV_PAD + k
            acc = acc + w[k:k + 1, :] * buf_ref[start:start + CONV_ROWS, :]
        mu = jnp.mean(acc, axis=-1, keepdims=True)
        xc = acc - mu
        var = jnp.mean(xc * xc, axis=-1, keepdims=True)
        y = xc * lax.rsqrt(var + EPS) * g_ref[...] + beta_ref[...]
        o_ref[0, r0:r0 + CONV_ROWS, :] = _silu(y).astype(o_ref.dtype)


def conv_module(pc, w, b, g, beta):
    bsz, s, _ = pc.shape
    nt = s // CONV_TILE
    hb = CONV_TILE // CONV_HALO
    last_halo = s // CONV_HALO - 1
    vec = lambda: pl.BlockSpec((1, CONV_CH), lambda i, t: (0, 0))
    return pl.pallas_call(
        _conv_kernel,
        grid=(bsz, nt),
        in_specs=[
            pl.BlockSpec((1, CONV_TILE, CONV_SLAB), lambda i, t: (i, t, 0)),
            pl.BlockSpec((1, CONV_HALO, CONV_SLAB), lambda i, t: (i, jnp.maximum(t * hb - 1, 0), 0)),
            pl.BlockSpec((1, CONV_HALO, CONV_SLAB),
                         lambda i, t: (i, jnp.minimum((t + 1) * hb, last_halo), 0)),
            pl.BlockSpec((CONV_WIDTH, CONV_CH), lambda i, t: (0, 0)),
            vec(), vec(), vec(),
        ],
        out_specs=pl.BlockSpec((1, CONV_TILE, CONV_CH), lambda i, t: (i, t, 0)),
        out_shape=jax.ShapeDtypeStruct((bsz, s, CONV_CH), BF16),
        scratch_shapes=[pltpu.VMEM((CONV_TILE + 2 * CONV_HALO, CONV_CH), F32)],
        compiler_params=_cparams(("parallel", "parallel")),
        name="conv_module",
    )(pc, pc, pc, w, b, g, beta)


def _ssd_kernel(*refs, reverse):
    if reverse:
        (cur_ref, left_ref, right_ref, cw_ref, cb_ref, dtb_ref, alog_ref, exp_ref,
         yf_ref, dskip_ref, ng_ref, o_ref, state_ref, buf_ref) = refs
    else:
        (cur_ref, left_ref, right_ref, cw_ref, cb_ref, dtb_ref, alog_ref, exp_ref,
         o_ref, state_ref, buf_ref) = refs
    q = SSD_CHUNK
    c = pl.program_id(1)
    nc = pl.num_programs(1)
    tc = (nc - 1 - c) if reverse else c

    @pl.when(c == 0)
    def _():
        state_ref[...] = jnp.zeros_like(state_ref)

    blk = cur_ref[0]
    xbc_lo, xbc_hi = SSD_WIDTH, SSD_WIDTH + SSD_XBC
    buf_ref[0:SSD_HALO, :] = jnp.where(tc > 0, left_ref[0][:, xbc_lo:xbc_hi], 0.0)
    buf_ref[SSD_HALO:SSD_HALO + q, :] = blk[:, xbc_lo:xbc_hi]
    buf_ref[SSD_HALO + q:, :] = jnp.where(tc < nc - 1, right_ref[0][:, xbc_lo:xbc_hi], 0.0)
    acc = jnp.broadcast_to(cb_ref[...], (q, SSD_XBC))
    cw = cw_ref[...]
    for k in range(SSD_CONV_WIDTH):
        start = SSD_HALO - SSD_CONV_PAD + k
        acc = acc + cw[k:k + 1, :] * buf_ref[start:start + q, :]
    xbc = _silu(acc)
    xs = xbc[:, :SSD_WIDTH]
    bmat = xbc[:, SSD_WIDTH:SSD_WIDTH + SSD_GROUPS * SSD_STATE]
    cmat = xbc[:, SSD_WIDTH + SSD_GROUPS * SSD_STATE:]

    dt = jax.nn.softplus(blk[:, xbc_hi:] + dtb_ref[...])
    dta = dt * (-jnp.exp(alog_ref[...]))
    row = lax.broadcasted_iota(jnp.int32, (q, q), 0)
    col = lax.broadcasted_iota(jnp.int32, (q, q), 1)
    keep = (row <= col) if reverse else (row >= col)
    a_cs = _dot_exact(keep.astype(F32), dta)
    a_tot = jnp.sum(dta, axis=0, keepdims=True)
    ea = jnp.exp(a_cs)
    dec = jnp.exp(a_tot - a_cs)
    cdec = jnp.broadcast_to(jnp.exp(a_tot), (SUBLANES, LANES))
    ex = _dot_exact(jnp.concatenate([dt, ea, dec, cdec], axis=0), exp_ref[...])
    dtx, eax, decx, cdecx = ex[0:q], ex[q:2 * q], ex[2 * q:3 * q], ex[3 * q:3 * q + 1]
    xdt = xs * dtx
    xdt_b = xdt.astype(BF16)
    xd_b = (xdt * decx).astype(BF16)
    a_cs_t = a_cs.T
    lane = lax.broadcasted_iota(jnp.int32, (q, LANES), 1)
    lo_half = lane < SSD_HEAD_DIM
    lane0 = SSD_HEADS if reverse else 0

    y_groups = []
    for g in range(SSD_GROUPS):
        gs = slice(g * SSD_GW, (g + 1) * SSD_GW)
        bg = bmat[:, g * SSD_STATE:(g + 1) * SSD_STATE]
        cg_b = cmat[:, g * SSD_STATE:(g + 1) * SSD_STATE].astype(BF16)
        cb = _dot_nt(cg_b, bg.astype(BF16))
        prev = state_ref[:, gs]
        y_off = _dot(cg_b, prev.astype(BF16)) * eax[:, gs]
        new_states = _dot(bg.T.astype(BF16), xd_b[:, gs])
        state_ref[:, gs] = prev * cdecx[:, gs] + new_states
        pairs = []
        for j in range(SSD_HPG // 2):
            xpair = xdt_b[:, g * SSD_GW + j * LANES:g * SSD_GW + (j + 1) * LANES]
            ypair = None
            for half in range(2):
                ln = lane0 + g * SSD_HPG + 2 * j + half
                seg = a_cs[:, ln:ln + 1] - a_cs_t[ln:ln + 1, :]
                decay = jnp.exp(jnp.where(keep, seg, -jnp.inf))
                mixer = (cb * decay).astype(BF16)
                xh = jnp.where(lo_half if half == 0 else jnp.logical_not(lo_half), xpair,
                               jnp.zeros_like(xpair))
                part = _dot(mixer, xh)
                ypair = part if ypair is None else ypair + part
            pairs.append(ypair)
        y_groups.append(jnp.concatenate(pairs, axis=1) + y_off)
    y = jnp.concatenate(y_groups, axis=1)

    if reverse:
        z = blk[:, :SSD_WIDTH]
        y = y + yf_ref[0] + dskip_ref[...] * xs
        o_ref[0] = _rms(y * _silu(z), ng_ref[...]).astype(o_ref.dtype)
    else:
        o_ref[0] = y


def ssd_scan(ps, cw, cb, dtb, alog, expand, *, reverse, y_fwd=None, dskip=None, ng=None):
    bsz, s, _ = ps.shape
    q = SSD_CHUNK
    nc = s // q
    hb = q // SSD_HALO
    last_halo = s // SSD_HALO - 1
    tpos = (lambda c: nc - 1 - c) if reverse else (lambda c: c)
    full = lambda shape: pl.BlockSpec(shape, lambda i, c: (0,) * len(shape))
    in_specs = [
        pl.BlockSpec((1, q, SSD_SLAB), lambda i, c: (i, tpos(c), 0)),
        pl.BlockSpec((1, SSD_HALO, SSD_SLAB), lambda i, c: (i, jnp.maximum(tpos(c) * hb - 1, 0), 0)),
        pl.BlockSpec((1, SSD_HALO, SSD_SLAB),
                     lambda i, c: (i, jnp.minimum((tpos(c) + 1) * hb, last_halo), 0)),
        full((SSD_CONV_WIDTH, SSD_XBC)), full((1, SSD_XBC)), full((1, LANES)), full((1, LANES)),
        full((LANES, SSD_WIDTH)),
    ]
    args = [ps, ps, ps, cw, cb, dtb, alog, expand]
    if reverse:
        in_specs += [pl.BlockSpec((1, q, SSD_WIDTH), lambda i, c: (i, tpos(c), 0)),
                     full((1, SSD_WIDTH)), full((1, SSD_WIDTH))]
        args += [y_fwd, dskip, ng]
    return pl.pallas_call(
        functools.partial(_ssd_kernel, reverse=reverse),
        grid=(bsz, nc),
        in_specs=in_specs,
        out_specs=pl.BlockSpec((1, q, SSD_WIDTH), lambda i, c: (i, tpos(c), 0)),
        out_shape=jax.ShapeDtypeStruct((bsz, s, SSD_WIDTH), BF16 if reverse else F32),
        scratch_shapes=[pltpu.VMEM((SSD_STATE, SSD_WIDTH), F32),
                        pltpu.VMEM((q + 2 * SSD_HALO, SSD_XBC), F32)],
        compiler_params=_cparams(("parallel", "arbitrary")),
        name="ssd_bwd" if reverse else "ssd_fwd",
    )(*args)


def _bias_tiles_kernel(bucket_ref, table_ref, o_ref):
    h = pl.program_id(0)
    bucket = bucket_ref[0]
    out = jnp.zeros(bucket.shape, F32)
    for b in range(REL_BUCKETS):
        out = jnp.where(bucket == b, table_ref[b, h], out)
    o_ref[0, 0] = out


def bias_tiles(buckets, rel_bias):
    nt, t, _ = buckets.shape
    return pl.pallas_call(
        _bias_tiles_kernel,
        grid=(ATT_HEADS, nt),
        in_specs=[pl.BlockSpec((1, t, t), lambda h, d: (d, 0, 0)),
                  pl.BlockSpec(memory_space=pltpu.SMEM)],
        out_specs=pl.BlockSpec((1, 1, t, t), lambda h, d: (h, d, 0, 0)),
        out_shape=jax.ShapeDtypeStruct((ATT_HEADS, nt, t, t), F32),
        compiler_params=_cparams(("parallel", "parallel")),
        name="bias_tiles",
    )(buckets, rel_bias)


def _rel_bucket(rel):
    nb = REL_BUCKETS // 2
    max_exact = nb // 2
    ret = jnp.where(rel > 0, nb, 0)
    n = jnp.abs(rel)
    nf = jnp.maximum(n, 1).astype(jnp.float32)
    large = max_exact + (jnp.log(nf / max_exact) / math.log(REL_MAX_DIST / max_exact)
                         * (nb - max_exact)).astype(jnp.int32)
    large = jnp.minimum(large, nb - 1)
    return ret + jnp.where(n < max_exact, n, large)


def _bucket_tiles(t):
    i = jnp.arange(t, dtype=jnp.int32)
    half = N_BIAS_TILES // 2
    offs = (jnp.arange(N_BIAS_TILES, dtype=jnp.int32) - half) * t
    rel = offs[:, None, None] + i[None, None, :] - i[None, :, None]
    return _rel_bucket(rel).astype(jnp.int32)


def _attn_kernel(q_ref, k_ref, v_ref, bias_ref, lam_ref, g_ref, o_ref,
                 m_ref, l_ref, acc_ref, *, lam_init, nk):
    t = ATT_TILE
    qi = pl.program_id(2)
    qb = q_ref[0] * (ATT_HEAD_DIM ** -0.5)
    lane = lax.broadcasted_iota(jnp.int32, (t, 2 * ATT_HEAD_DIM), 1)
    map_lanes = (lane < ATT_HEAD_DIM, lane >= ATT_HEAD_DIM)
    m_ref[...] = jnp.full(m_ref.shape, -jnp.inf, F32)
    l_ref[...] = jnp.zeros_like(l_ref)
    acc_ref[...] = jnp.zeros_like(acc_ref)
    half = N_BIAS_TILES // 2

    def step(kj, carry):
        ks = pl.multiple_of(kj * t, t)
        kb = k_ref[0, pl.ds(ks, t), :]
        vb = v_ref[0, pl.ds(ks, t), :]
        bias = bias_ref[0, jnp.clip(kj - qi, -half, half) + half]
        for mp in range(2):
            kz = jnp.where(map_lanes[mp], kb, jnp.zeros_like(kb))
            s = _dot_nt(qb, kz) + bias
            m_old = m_ref[mp]
            m_new = jnp.maximum(m_old, jnp.max(s, axis=-1, keepdims=True))
            alpha = jnp.exp(m_old - m_new)
            p = jnp.exp(s - m_new)
            l_ref[mp] = alpha * l_ref[mp] + jnp.sum(p, axis=-1, keepdims=True)
            acc_ref[mp] = alpha * acc_ref[mp] + _dot(p.astype(BF16), vb)
            m_ref[mp] = m_new
        return carry

    lax.fori_loop(0, nk, step, 0)

    lp = lam_ref[...]
    lam = (jnp.exp(jnp.sum(lp[0:1] * lp[1:2], axis=-1, keepdims=True))
           - jnp.exp(jnp.sum(lp[2:3] * lp[3:4], axis=-1, keepdims=True)) + lam_init)
    o = acc_ref[0] / l_ref[0] - lam * (acc_ref[1] / l_ref[1])
    o_ref[0] = (_rms(o, g_ref[...]) * (1.0 - lam_init)).astype(o_ref.dtype)


def diff_attention(qkv, bias, lam_params, subln_g, lam_init):
    bsz, s, _ = qkv.shape
    t = ATT_TILE
    nq = s // t
    hw = 2 * ATT_HEAD_DIM
    return pl.pallas_call(
        functools.partial(_attn_kernel, lam_init=lam_init, nk=nq),
        grid=(bsz, ATT_HEADS, nq),
        in_specs=[
            pl.BlockSpec((1, t, hw), lambda b, h, i: (b, i, h)),
            pl.BlockSpec((1, s, hw), lambda b, h, i: (b, 0, ATT_HEADS + h)),
            pl.BlockSpec((1, s, hw), lambda b, h, i: (b, 0, 2 * ATT_HEADS + h)),
            pl.BlockSpec((1, N_BIAS_TILES, t, t), lambda b, h, i: (h, 0, 0, 0)),
            pl.BlockSpec((4, ATT_HEAD_DIM), lambda b, h, i: (0, 0)),
            pl.BlockSpec((1, hw), lambda b, h, i: (0, 0)),
        ],
        out_specs=pl.BlockSpec((1, t, hw), lambda b, h, i: (b, i, h)),
        out_shape=jax.ShapeDtypeStruct((bsz, s, ATT_WIDTH), BF16),
        scratch_shapes=[pltpu.VMEM((2, t, 1), F32), pltpu.VMEM((2, t, 1), F32),
                        pltpu.VMEM((2, t, hw), F32)],
        compiler_params=_cparams(("parallel", "parallel", "parallel")),
        name="diff_attention",
    )(qkv, qkv, qkv, bias, lam_params, subln_g)


def _out_proj_kernel(yc_ref, ys_ref, ya_ref, h_ref, w_ref, o_ref):
    c0, c1 = CONV_CH, CONV_CH + SSD_WIDTH
    acc = _dot(yc_ref[...], w_ref[0:c0, :])
    acc = acc + _dot(ys_ref[...], w_ref[c0:c1, :])
    acc = acc + _dot(ya_ref[...], w_ref[c1:, :])
    o_ref[...] = h_ref[...] + acc


def out_proj(yc, ys, ya, h, w, tm=ROW_TILE):
    m, d = h.shape
    row = lambda n: pl.BlockSpec((tm, n), lambda i: (i, 0))
    return pl.pallas_call(
        _out_proj_kernel,
        grid=(m // tm,),
        in_specs=[row(CONV_CH), row(SSD_WIDTH), row(ATT_WIDTH), row(d),
                  pl.BlockSpec(w.shape, lambda i: (0, 0))],
        out_specs=row(d),
        out_shape=jax.ShapeDtypeStruct((m, d), F32),
        compiler_params=_cparams(("parallel",)),
        name="out_proj",
    )(yc, ys, ya, h, w)


def _mlp_kernel(h_ref, g_ref, wu_ref, wd_ref, o_ref, xn_ref, acc_ref):
    f = pl.program_id(1)

    @pl.when(f == 0)
    def _():
        xn_ref[...] = _rms(h_ref[...], g_ref[...]).astype(BF16)
        acc_ref[...] = jnp.zeros_like(acc_ref)

    hid = jnp.square(jnp.maximum(_dot(xn_ref[...], wu_ref[...]), 0.0)).astype(BF16)
    acc_ref[...] += _dot(hid, wd_ref[...])

    @pl.when(f == pl.num_programs(1) - 1)
    def _():
        o_ref[...] = h_ref[...] + acc_ref[...]


def mlp(h, g, wu, wd, tm=ROW_TILE, tf=FF_TILE):
    m, d = h.shape
    ff = wu.shape[1]
    return pl.pallas_call(
        _mlp_kernel,
        grid=(m // tm, ff // tf),
        in_specs=[pl.BlockSpec((tm, d), lambda i, f: (i, 0)),
                  pl.BlockSpec((1, d), lambda i, f: (0, 0)),
                  pl.BlockSpec((d, tf), lambda i, f: (0, f)),
                  pl.BlockSpec((tf, d), lambda i, f: (f, 0))],
        out_specs=pl.BlockSpec((tm, d), lambda i, f: (i, 0)),
        out_shape=jax.ShapeDtypeStruct((m, d), F32),
        scratch_shapes=[pltpu.VMEM((tm, d), BF16), pltpu.VMEM((tm, d), F32)],
        compiler_params=_cparams(("parallel", "arbitrary")),
        name="mlp",
    )(h, g, wu, wd)


def _ple_kernel(h_ref, p_ref, g_ref, wg_ref, wp_ref, gf_ref, o_ref, *, final):
    h = h_ref[...]
    gate = jax.nn.sigmoid(_dot(_rms(h, g_ref[...]).astype(BF16), wg_ref[...]))
    out = h + _dot(p_ref[...].astype(BF16), wp_ref[...]) * gate
    if final:
        out = _rms(out, gf_ref[...])
    o_ref[...] = out


def ple(h, p, g, wg, wp, gf, final, tm=ROW_TILE):
    m, d = h.shape
    vec = pl.BlockSpec((1, d), lambda i: (0, 0))
    return pl.pallas_call(
        functools.partial(_ple_kernel, final=final),
        grid=(m // tm,),
        in_specs=[pl.BlockSpec((tm, d), lambda i: (i, 0)),
                  pl.BlockSpec((tm, PLE_DIM), lambda i: (i, 0)),
                  vec,
                  pl.BlockSpec((d, d), lambda i: (0, 0)),
                  pl.BlockSpec((PLE_DIM, d), lambda i: (0, 0)),
                  vec],
        out_specs=pl.BlockSpec((tm, d), lambda i: (i, 0)),
        out_shape=jax.ShapeDtypeStruct((m, d), F32),
        compiler_params=_cparams(("parallel",)),
        name="ple_final" if final else "ple",
    )(h, p, g, wg, wp, gf)


def _expand_matrix(direction):
    e = np.zeros((LANES, SSD_WIDTH), np.float32)
    for hd in range(SSD_HEADS):
        e[direction * SSD_HEADS + hd, hd * SSD_HEAD_DIM:(hd + 1) * SSD_HEAD_DIM] = 1.0
    return jnp.asarray(e)


def _pad_lanes(v):
    return jnp.pad(v.reshape(1, -1), ((0, 0), (0, LANES - v.size)))


def _layer_params(i, norm_mix_g, w_in, conv_w, conv_b, conv_norm_g, conv_norm_b, ssd_conv_w,
                  ssd_conv_b, ssd_dt_bias, ssd_a_log, ssd_d, ssd_norm_g, lambda_q1, lambda_k1,
                  lambda_q2, lambda_k2, attn_subln_g, w_out, norm_mlp_g, w_up, w_down,
                  norm_ple_g, w_ple, w_ple_gate):
    row = lambda v: v.reshape(1, -1)
    w = w_in[i]
    c_conv = 2 * CONV_CH
    c_dt = c_conv + SSD_WIDTH + SSD_XBC
    c_att = c_dt + 2 * SSD_HEADS
    w_ssd = jnp.pad(w[:, c_conv:c_att], ((0, 0), (0, SSD_DT_COLS - 2 * SSD_HEADS)))
    return dict(
        norm_mix_g=row(norm_mix_g[i]),
        w_conv=w[:, :c_conv].astype(BF16),
        w_ssd=w_ssd.astype(BF16),
        w_att=w[:, c_att:].astype(BF16),
        conv_w=conv_w[i], conv_b=row(conv_b[i]), conv_g=row(conv_norm_g[i]),
        conv_beta=row(conv_norm_b[i]),
        ssd_cw=ssd_conv_w[i], ssd_cb=row(ssd_conv_b[i]),
        ssd_dtb=_pad_lanes(ssd_dt_bias[i]), ssd_alog=_pad_lanes(ssd_a_log[i]),
        ssd_dskip=row(jnp.repeat(ssd_d[i], SSD_HEAD_DIM)), ssd_ng=row(ssd_norm_g[i]),
        lam=jnp.stack([lambda_q1[i], lambda_k1[i], lambda_q2[i], lambda_k2[i]]),
        subln_g=row(attn_subln_g[i]),
        w_out=w_out[i].astype(BF16),
        norm_mlp_g=row(norm_mlp_g[i]),
        w_up=w_up[i].astype(BF16), w_down=w_down[i].astype(BF16),
        norm_ple_g=row(norm_ple_g[i]),
        w_ple=w_ple[i].astype(BF16), w_gate=w_ple_gate[i].astype(BF16),
    )


def _run(x, p, layers, bias, expand_f, expand_b, final_g):
    bsz, s, d = x.shape
    m = bsz * s
    h = x.reshape(m, d)
    for i, lp in enumerate(layers):
        lam_init = 0.8 - 0.6 * math.exp(-0.3 * i)
        pc = norm_proj(h, lp["norm_mix_g"], lp["w_conv"], F32).reshape(bsz, s, CONV_SLAB)
        ps = norm_proj(h, lp["norm_mix_g"], lp["w_ssd"], F32).reshape(bsz, s, SSD_SLAB)
        pa = norm_proj(h, lp["norm_mix_g"], lp["w_att"], BF16).reshape(bsz, s, ATT_SLAB)
        y_conv = conv_module(pc, lp["conv_w"], lp["conv_b"], lp["conv_g"], lp["conv_beta"])
        y_f = ssd_scan(ps, lp["ssd_cw"], lp["ssd_cb"], lp["ssd_dtb"], lp["ssd_alog"], expand_f,
                       reverse=False)
        y_ssd = ssd_scan(ps, lp["ssd_cw"], lp["ssd_cb"], lp["ssd_dtb"], lp["ssd_alog"], expand_b,
                         reverse=True, y_fwd=y_f, dskip=lp["ssd_dskip"], ng=lp["ssd_ng"])
        y_att = diff_attention(pa, bias, lp["lam"], lp["subln_g"], lam_init)
        h = out_proj(y_conv.reshape(m, CONV_CH), y_ssd.reshape(m, SSD_WIDTH),
                     y_att.reshape(m, ATT_WIDTH), h, lp["w_out"])
        h = mlp(h, lp["norm_mlp_g"], lp["w_up"], lp["w_down"])
        h = ple(h, p[i].reshape(m, PLE_DIM), lp["norm_ple_g"], lp["w_gate"], lp["w_ple"],
                final_g, final=(i == len(layers) - 1))
    return h.reshape(bsz, s, d)


def kernel(x_prompt, x_sample, p_prompt, p_sample, norm_mix_g, w_in, conv_w, conv_b, conv_norm_g,
           conv_norm_b, ssd_conv_w, ssd_conv_b, ssd_dt_bias, ssd_a_log, ssd_d, ssd_norm_g,
           lambda_q1, lambda_k1, lambda_q2, lambda_k2, attn_subln_g, rel_bias, w_out, norm_mlp_g,
           w_up, w_down, norm_ple_g, w_ple, w_ple_gate, final_norm_g):
    layers = [_layer_params(i, norm_mix_g, w_in, conv_w, conv_b, conv_norm_g, conv_norm_b,
                            ssd_conv_w, ssd_conv_b, ssd_dt_bias, ssd_a_log, ssd_d, ssd_norm_g,
                            lambda_q1, lambda_k1, lambda_q2, lambda_k2, attn_subln_g, w_out,
                            norm_mlp_g, w_up, w_down, norm_ple_g, w_ple, w_ple_gate)
              for i in range(DEPTH)]
    bias = bias_tiles(_bucket_tiles(ATT_TILE), rel_bias)
    expand_f, expand_b = _expand_matrix(0), _expand_matrix(1)
    final_g = final_norm_g.reshape(1, -1)
    y_prompt = _run(x_prompt, p_prompt, layers, bias, expand_f, expand_b, final_g)
    y_sample = _run(x_sample, p_sample, layers, bias, expand_f, expand_b, final_g)
    return (y_prompt, y_sample)
```

```python
import functools
import math

import jax
import jax.numpy as jnp
import numpy as np
from jax import lax
from jax.experimental import pallas as pl
from jax.experimental.pallas import tpu as pltpu

F32 = jnp.float32
BF16 = jnp.bfloat16

D_MODEL = 2048
DEPTH = 2
PLE_DIM = 256
EPS = 1e-6
CONV_CH = 512
CONV_WIDTH = 31
CONV_PAD = (CONV_WIDTH - 1) // 2
SSD_HEAD_DIM = 64
SSD_HEADS = 12
SSD_WIDTH = SSD_HEADS * SSD_HEAD_DIM
SSD_GROUPS = 2
SSD_HPG = SSD_HEADS // SSD_GROUPS
SSD_STATE = 128
SSD_CONV_WIDTH = 5
SSD_CONV_PAD = (SSD_CONV_WIDTH - 1) // 2
SSD_CHUNK = 128
SSD_XBC = SSD_WIDTH + 2 * SSD_GROUPS * SSD_STATE
ATT_HEADS = 6
ATT_HEAD_DIM = 64
ATT_WIDTH = ATT_HEADS * 2 * ATT_HEAD_DIM
REL_BUCKETS = 32
REL_MAX_DIST = 128
D_FF = 4 * D_MODEL
LOG2E = math.log2(math.e)

LANES = 128
SUBLANES = 8
VMEM_LIMIT = 56 * 1024 * 1024

SSD_DT_COLS = LANES
SSD_SLAB = SSD_WIDTH + SSD_XBC + SSD_DT_COLS
CONV_SLAB = 2 * CONV_CH
ATT_SLAB = 3 * ATT_WIDTH
SSD_GW = SSD_HPG * SSD_HEAD_DIM

ATT_TILE = 512
N_BIAS_TILES = 5
ATT_ROWS = 32
CONV_TILE = 128
CONV_HALO = 16
CONV_ROWS = 32
SSD_HALO = SUBLANES
ROW_TILE = 512
FF_TILE = 1024


def _cparams(sem):
    return pltpu.CompilerParams(dimension_semantics=sem, vmem_limit_bytes=VMEM_LIMIT)


def _rms(x, g):
    ms = jnp.mean(x * x, axis=-1, keepdims=True)
    return (x * lax.rsqrt(ms + EPS)) * g


def _silu(x):
    return x * jax.nn.sigmoid(x)


def _dot(a, b):
    return jnp.dot(a, b, preferred_element_type=F32)


def _dot_exact(a, b):
    return jnp.dot(a, b, preferred_element_type=F32, precision=lax.Precision.HIGHEST)


def _dot_nt(a, b):
    return lax.dot_general(a, b, (((1,), (1,)), ((), ())), preferred_element_type=F32)


def _norm_proj_kernel(x_ref, g_ref, w_ref, o_ref):
    xn = _rms(x_ref[...], g_ref[...]).astype(BF16)
    o_ref[...] = _dot(xn, w_ref[...]).astype(o_ref.dtype)


def norm_proj(x, g, w, out_dtype, tm=ROW_TILE):
    m, d = x.shape
    n = w.shape[1]
    return pl.pallas_call(
        _norm_proj_kernel,
        grid=(m // tm,),
        in_specs=[pl.BlockSpec((tm, d), lambda i: (i, 0)),
                  pl.BlockSpec((1, d), lambda i: (0, 0)),
                  pl.BlockSpec((d, n), lambda i: (0, 0))],
        out_specs=pl.BlockSpec((tm, n), lambda i: (i, 0)),
        out_shape=jax.ShapeDtypeStruct((m, n), out_dtype),
        compiler_params=_cparams(("parallel",)),
        name="norm_proj",
    )(x, g, w)


def _conv_kernel(cur_ref, left_ref, right_ref, w_ref, b_ref, g_ref, beta_ref, o_ref, buf_ref):
    t = pl.program_id(1)
    nt = pl.num_programs(1)

    def glu(blk):
        return blk[:, :CONV_CH] * jax.nn.sigmoid(blk[:, CONV_CH:])

    buf_ref[0:CONV_HALO, :] = jnp.where(t > 0, glu(left_ref[0]), 0.0)
    buf_ref[CONV_HALO:CONV_HALO + CONV_TILE, :] = glu(cur_ref[0])
    buf_ref[CONV_HALO + CONV_TILE:, :] = jnp.where(t < nt - 1, glu(right_ref[0]), 0.0)

    w = w_ref[...]
    for r0 in range(0, CONV_TILE, CONV_ROWS):
        acc = jnp.broadcast_to(b_ref[...], (CONV_ROWS, CONV_CH))
        for k in range(CONV_WIDTH):
            start = r0 + CONV_HALO - CONV_PAD + k
            acc = acc + w[k:k + 1, :] * buf_ref[start:start + CONV_ROWS, :]
        mu = jnp.mean(acc, axis=-1, keepdims=True)
        xc = acc - mu
        var = jnp.mean(xc * xc, axis=-1, keepdims=True)
        y = xc * lax.rsqrt(var + EPS) * g_ref[...] + beta_ref[...]
        o_ref[0, r0:r0 + CONV_ROWS, :] = _silu(y).astype(o_ref.dtype)


def conv_module(pc, w, b, g, beta):
    bsz, s, _ = pc.shape
    nt = s // CONV_TILE
    hb = CONV_TILE // CONV_HALO
    last_halo = s // CONV_HALO - 1
    vec = lambda: pl.BlockSpec((1, CONV_CH), lambda i, t: (0, 0))
    return pl.pallas_call(
        _conv_kernel,
        grid=(bsz, nt),
        in_specs=[
            pl.BlockSpec((1, CONV_TILE, CONV_SLAB), lambda i, t: (i, t, 0)),
            pl.BlockSpec((1, CONV_HALO, CONV_SLAB), lambda i, t: (i, jnp.maximum(t * hb - 1, 0), 0)),
            pl.BlockSpec((1, CONV_HALO, CONV_SLAB),
                         lambda i, t: (i, jnp.minimum((t + 1) * hb, last_halo), 0)),
            pl.BlockSpec((CONV_WIDTH, CONV_CH), lambda i, t: (0, 0)),
            vec(), vec(), vec(),
        ],
        out_specs=pl.BlockSpec((1, CONV_TILE, CONV_CH), lambda i, t: (i, t, 0)),
        out_shape=jax.ShapeDtypeStruct((bsz, s, CONV_CH), BF16),
        scratch_shapes=[pltpu.VMEM((CONV_TILE + 2 * CONV_HALO, CONV_CH), F32)],
        compiler_params=_cparams(("parallel", "parallel")),
        name="conv_module",
    )(pc, pc, pc, w, b, g, beta)


def _ssd_kernel(*refs, reverse):
    if reverse:
        (cur_ref, left_ref, right_ref, cw_ref, cb_ref, dtb_ref, alog_ref, exp_ref,
         yf_ref, dskip_ref, ng_ref, o_ref, state_ref, buf_ref) = refs
    else:
        (cur_ref, left_ref, right_ref, cw_ref, cb_ref, dtb_ref, alog_ref, exp_ref,
         o_ref, state_ref, buf_ref) = refs
    q = SSD_CHUNK
    c = pl.program_id(1)
    nc = pl.num_programs(1)
    tc = (nc - 1 - c) if reverse else c

    @pl.when(c == 0)
    def _():
        state_ref[...] = jnp.zeros_like(state_ref)

    blk = cur_ref[0]
    xbc_lo, xbc_hi = SSD_WIDTH, SSD_WIDTH + SSD_XBC
    buf_ref[0:SSD_HALO, :] = jnp.where(tc > 0, left_ref[0][:, xbc_lo:xbc_hi], 0.0)
    buf_ref[SSD_HALO:SSD_HALO + q, :] = blk[:, xbc_lo:xbc_hi]
    buf_ref[SSD_HALO + q:, :] = jnp.where(tc < nc - 1, right_ref[0][:, xbc_lo:xbc_hi], 0.0)
    acc = jnp.broadcast_to(cb_ref[...], (q, SSD_XBC))
    cw = cw_ref[...]
    for k in range(SSD_CONV_WIDTH):
        start = SSD_HALO - SSD_CONV_PAD + k
        acc = acc + cw[k:k + 1, :] * buf_ref[start:start + q, :]
    xbc = _silu(acc)
    xs = xbc[:, :SSD_WIDTH]
    bmat = xbc[:, SSD_WIDTH:SSD_WIDTH + SSD_GROUPS * SSD_STATE]
    cmat = xbc[:, SSD_WIDTH + SSD_GROUPS * SSD_STATE:]

    dt = jax.nn.softplus(blk[:, xbc_hi:] + dtb_ref[...])
    dta = dt * (-jnp.exp(alog_ref[...]))
    row = lax.broadcasted_iota(jnp.int32, (q, q), 0)
    col = lax.broadcasted_iota(jnp.int32, (q, q), 1)
    keep = (row <= col) if reverse else (row >= col)
    a_cs = _dot_exact(keep.astype(F32), dta)
    a_tot = jnp.sum(dta, axis=0, keepdims=True)
    ea = jnp.exp(a_cs)
    dec = jnp.exp(a_tot - a_cs)
    cdec = jnp.broadcast_to(jnp.exp(a_tot), (SUBLANES, LANES))
    ex = _dot_exact(jnp.concatenate([dt, ea, dec, cdec], axis=0), exp_ref[...])
    dtx, eax, decx, cdecx = ex[0:q], ex[q:2 * q], ex[2 * q:3 * q], ex[3 * q:3 * q + 1]
    xdt = xs * dtx
    xdt_b = xdt.astype(BF16)
    xd_b = (xdt * decx).astype(BF16)
    a_cs_t = a_cs.T
    lane = lax.broadcasted_iota(jnp.int32, (q, LANES), 1)
    lo_half = lane < SSD_HEAD_DIM
    lane0 = SSD_HEADS if reverse else 0

    y_groups = []
    for g in range(SSD_GROUPS):
        gs = slice(g * SSD_GW, (g + 1) * SSD_GW)
        bg = bmat[:, g * SSD_STATE:(g + 1) * SSD_STATE]
        cg_b = cmat[:, g * SSD_STATE:(g + 1) * SSD_STATE].astype(BF16)
        cb = _dot_nt(cg_b, bg.astype(BF16))
        prev = state_ref[:, gs]
        y_off = _dot(cg_b, prev.astype(BF16)) * eax[:, gs]
        new_states = _dot(bg.T.astype(BF16), xd_b[:, gs])
        state_ref[:, gs] = prev * cdecx[:, gs] + new_states
        pairs = []
        for j in range(SSD_HPG // 2):
            xpair = xdt_b[:, g * SSD_GW + j * LANES:g * SSD_GW + (j + 1) * LANES]
            ypair = None
            for half in range(2):
                ln = lane0 + g * SSD_HPG + 2 * j + half
                seg = a_cs[:, ln:ln + 1] - a_cs_t[ln:ln + 1, :]
                decay = jnp.exp(jnp.where(keep, seg, -jnp.inf))
                mixer = (cb * decay).astype(BF16)
                xh = jnp.where(lo_half if half == 0 else jnp.logical_not(lo_half), xpair,
                               jnp.zeros_like(xpair))
                part = _dot(mixer, xh)
                ypair = part if ypair is None else ypair + part
            pairs.append(ypair)
        y_groups.append(jnp.concatenate(pairs, axis=1) + y_off)
    y = jnp.concatenate(y_groups, axis=1)

    if reverse:
        z = blk[:, :SSD_WIDTH]
        y = y + yf_ref[0] + dskip_ref[...] * xs
        o_ref[0] = _rms(y * _silu(z), ng_ref[...]).astype(o_ref.dtype)
    else:
        o_ref[0] = y


def ssd_scan(ps, cw, cb, dtb, alog, expand, *, reverse, y_fwd=None, dskip=None, ng=None):
    bsz, s, _ = ps.shape
    q = SSD_CHUNK
    nc = s // q
    hb = q // SSD_HALO
    last_halo = s // SSD_HALO - 1
    tpos = (lambda c: nc - 1 - c) if reverse else (lambda c: c)
    full = lambda shape: pl.BlockSpec(shape, lambda i, c: (0,) * len(shape))
    in_specs = [
        pl.BlockSpec((1, q, SSD_SLAB), lambda i, c: (i, tpos(c), 0)),
        pl.BlockSpec((1, SSD_HALO, SSD_SLAB), lambda i, c: (i, jnp.maximum(tpos(c) * hb - 1, 0), 0)),
        pl.BlockSpec((1, SSD_HALO, SSD_SLAB),
                     lambda i, c: (i, jnp.minimum((tpos(c) + 1) * hb, last_halo), 0)),
        full((SSD_CONV_WIDTH, SSD_XBC)), full((1, SSD_XBC)), full((1, LANES)), full((1, LANES)),
        full((LANES, SSD_WIDTH)),
    ]
    args = [ps, ps, ps, cw, cb, dtb, alog, expand]
    if reverse:
        in_specs += [pl.BlockSpec((1, q, SSD_WIDTH), lambda i, c: (i, tpos(c), 0)),
                     full((1, SSD_WIDTH)), full((1, SSD_WIDTH))]
        args += [y_fwd, dskip, ng]
    return pl.pallas_call(
        functools.partial(_ssd_kernel, reverse=reverse),
        grid=(bsz, nc),
        in_specs=in_specs,
        out_specs=pl.BlockSpec((1, q, SSD_WIDTH), lambda i, c: (i, tpos(c), 0)),
        out_shape=jax.ShapeDtypeStruct((bsz, s, SSD_WIDTH), BF16 if reverse else F32),
        scratch_shapes=[pltpu.VMEM((SSD_STATE, SSD_WIDTH), F32),
                        pltpu.VMEM((q + 2 * SSD_HALO, SSD_XBC), F32)],
        compiler_params=_cparams(("parallel", "arbitrary")),
        name="ssd_bwd" if reverse else "ssd_fwd",
    )(*args)


def _bias_tiles_kernel(bucket_ref, table_ref, o_ref):
    h = pl.program_id(0)
    bucket = bucket_ref[0]
    out = jnp.zeros(bucket.shape, F32)
    for b in range(REL_BUCKETS):
        out = jnp.where(bucket == b, table_ref[b, h] * LOG2E, out)
    o_ref[0, 0] = out


def bias_tiles(buckets, rel_bias):
    nt, t, _ = buckets.shape
    return pl.pallas_call(
        _bias_tiles_kernel,
        grid=(ATT_HEADS, nt),
        in_specs=[pl.BlockSpec((1, t, t), lambda h, d: (d, 0, 0)),
                  pl.BlockSpec(memory_space=pltpu.SMEM)],
        out_specs=pl.BlockSpec((1, 1, t, t), lambda h, d: (h, d, 0, 0)),
        out_shape=jax.ShapeDtypeStruct((ATT_HEADS, nt, t, t), F32),
        compiler_params=_cparams(("parallel", "parallel")),
        name="bias_tiles",
    )(buckets, rel_bias)


def _rel_bucket(rel):
    nb = REL_BUCKETS // 2
    max_exact = nb // 2
    ret = jnp.where(rel > 0, nb, 0)
    n = jnp.abs(rel)
    nf = jnp.maximum(n, 1).astype(jnp.float32)
    large = max_exact + (jnp.log(nf / max_exact) / math.log(REL_MAX_DIST / max_exact)
                         * (nb - max_exact)).astype(jnp.int32)
    large = jnp.minimum(large, nb - 1)
    return ret + jnp.where(n < max_exact, n, large)


def _bucket_tiles(t):
    i = jnp.arange(t, dtype=jnp.int32)
    half = N_BIAS_TILES // 2
    offs = (jnp.arange(N_BIAS_TILES, dtype=jnp.int32) - half) * t
    rel = offs[:, None, None] + i[None, None, :] - i[None, :, None]
    return _rel_bucket(rel).astype(jnp.int32)


def _attn_kernel(q_ref, k_ref, v_ref, bias_ref, lam_ref, g_ref, o_ref,
                 qz_ref, s_ref, p_ref, m_ref, l_ref, alpha_ref, acc_ref, *, lam_init, nk):
    t = ATT_TILE
    qi = pl.program_id(2)
    qb = q_ref[0]
    lane = lax.broadcasted_iota(jnp.int32, (t, 2 * ATT_HEAD_DIM), 1)
    qz_ref[0] = jnp.where(lane < ATT_HEAD_DIM, qb, jnp.zeros_like(qb))
    qz_ref[1] = jnp.where(lane >= ATT_HEAD_DIM, qb, jnp.zeros_like(qb))
    m_ref[...] = jnp.full(m_ref.shape, -jnp.inf, F32)
    l_ref[...] = jnp.zeros_like(l_ref)
    acc_ref[...] = jnp.zeros_like(acc_ref)
    p_ref[1] = jnp.zeros(p_ref.shape[1:], p_ref.dtype)
    alpha_ref[1] = jnp.ones(alpha_ref.shape[1:], F32)
    half = N_BIAS_TILES // 2
    rb = ATT_ROWS

    def apply_pv(slot, kj):
        vb = v_ref[0, pl.ds(pl.multiple_of(kj * t, t), t), :]
        for mp in range(2):
            acc_ref[mp] = alpha_ref[slot, mp] * acc_ref[mp] + _dot(p_ref[slot, mp], vb)

    def pair(kk, carry):
        for slot in range(2):
            kj = 2 * kk + slot
            kb = k_ref[0, pl.ds(pl.multiple_of(kj * t, t), t), :]
            tile = jnp.clip(kj - qi, -half, half) + half
            for mp in range(2):
                s_ref[slot, mp] = _dot_nt(qz_ref[mp], kb)
            apply_pv(1 - slot, jnp.maximum(kj - 1, 0))
            for mp in range(2):
                for r0 in range(0, t, rb):
                    rows = slice(r0, r0 + rb)
                    s = s_ref[slot, mp, rows, :] + bias_ref[0, tile, rows, :]
                    m_old = m_ref[mp, rows, :]
                    m_new = jnp.maximum(m_old, jnp.max(s, axis=-1, keepdims=True))
                    alpha = jnp.exp2(m_old - m_new)
                    p = jnp.exp2(s - jnp.concatenate([m_new] * (t // LANES), axis=1))
                    l_ref[mp, rows, :] = (alpha * l_ref[mp, rows, :]
                                          + jnp.sum(p, axis=-1, keepdims=True))
                    m_ref[mp, rows, :] = m_new
                    alpha_ref[slot, mp, rows, :] = alpha
                    p_ref[slot, mp, rows, :] = p.astype(BF16)
        return carry

    lax.fori_loop(0, nk // 2, pair, 0)
    apply_pv(1, nk - 1)

    lp = lam_ref[...]
    lam = (jnp.exp(jnp.sum(lp[0:1] * lp[1:2], axis=-1, keepdims=True))
           - jnp.exp(jnp.sum(lp[2:3] * lp[3:4], axis=-1, keepdims=True)) + lam_init)
    o = acc_ref[0] / l_ref[0] - lam * (acc_ref[1] / l_ref[1])
    o_ref[0] = (_rms(o, g_ref[...]) * (1.0 - lam_init)).astype(o_ref.dtype)


def diff_attention(qkv, bias, lam_params, subln_g, lam_init):
    bsz, s, _ = qkv.shape
    t = ATT_TILE
    nq = s // t
    hw = 2 * ATT_HEAD_DIM
    assert nq % 2 == 0
    stat = pltpu.VMEM((2, t, LANES), F32)
    return pl.pallas_call(
        functools.partial(_attn_kernel, lam_init=lam_init, nk=nq),
        grid=(bsz, ATT_HEADS, nq),
        in_specs=[
            pl.BlockSpec((1, t, hw), lambda b, h, i: (b, i, h)),
            pl.BlockSpec((1, s, hw), lambda b, h, i: (b, 0, ATT_HEADS + h)),
            pl.BlockSpec((1, s, hw), lambda b, h, i: (b, 0, 2 * ATT_HEADS + h)),
            pl.BlockSpec((1, N_BIAS_TILES, t, t), lambda b, h, i: (h, 0, 0, 0)),
            pl.BlockSpec((4, ATT_HEAD_DIM), lambda b, h, i: (0, 0)),
            pl.BlockSpec((1, hw), lambda b, h, i: (0, 0)),
        ],
        out_specs=pl.BlockSpec((1, t, hw), lambda b, h, i: (b, i, h)),
        out_shape=jax.ShapeDtypeStruct((bsz, s, ATT_WIDTH), BF16),
        scratch_shapes=[pltpu.VMEM((2, t, hw), BF16),
                        pltpu.VMEM((2, 2, t, t), F32),
                        pltpu.VMEM((2, 2, t, t), BF16),
                        stat, stat,
                        pltpu.VMEM((2, 2, t, LANES), F32),
                        pltpu.VMEM((2, t, hw), F32)],
        compiler_params=_cparams(("parallel", "parallel", "parallel")),
        name="diff_attention",
    )(qkv, qkv, qkv, bias, lam_params, subln_g)


def _out_proj_kernel(yc_ref, ys_ref, ya_ref, h_ref, w_ref, o_ref):
    c0, c1 = CONV_CH, CONV_CH + SSD_WIDTH
    acc = _dot(yc_ref[...], w_ref[0:c0, :])
    acc = acc + _dot(ys_ref[...], w_ref[c0:c1, :])
    acc = acc + _dot(ya_ref[...], w_ref[c1:, :])
    o_ref[...] = h_ref[...] + acc


def out_proj(yc, ys, ya, h, w, tm=ROW_TILE):
    m, d = h.shape
    row = lambda n: pl.BlockSpec((tm, n), lambda i: (i, 0))
    return pl.pallas_call(
        _out_proj_kernel,
        grid=(m // tm,),
        in_specs=[row(CONV_CH), row(SSD_WIDTH), row(ATT_WIDTH), row(d),
                  pl.BlockSpec(w.shape, lambda i: (0, 0))],
        out_specs=row(d),
        out_shape=jax.ShapeDtypeStruct((m, d), F32),
        compiler_params=_cparams(("parallel",)),
        name="out_proj",
    )(yc, ys, ya, h, w)


def _mlp_kernel(h_ref, g_ref, wu_ref, wd_ref, o_ref, xn_ref, acc_ref):
    f = pl.program_id(1)

    @pl.when(f == 0)
    def _():
        xn_ref[...] = _rms(h_ref[...], g_ref[...]).astype(BF16)
        acc_ref[...] = jnp.zeros_like(acc_ref)

    hid = jnp.square(jnp.maximum(_dot(xn_ref[...], wu_ref[...]), 0.0)).astype(BF16)
    acc_ref[...] += _dot(hid, wd_ref[...])

    @pl.when(f == pl.num_programs(1) - 1)
    def _():
        o_ref[...] = h_ref[...] + acc_ref[...]


def mlp(h, g, wu, wd, tm=ROW_TILE, tf=FF_TILE):
    m, d = h.shape
    ff = wu.shape[1]
    return pl.pallas_call(
        _mlp_kernel,
        grid=(m // tm, ff // tf),
        in_specs=[pl.BlockSpec((tm, d), lambda i, f: (i, 0)),
                  pl.BlockSpec((1, d), lambda i, f: (0, 0)),
                  pl.BlockSpec((d, tf), lambda i, f: (0, f)),
                  pl.BlockSpec((tf, d), lambda i, f: (f, 0))],
        out_specs=pl.BlockSpec((tm, d), lambda i, f: (i, 0)),
        out_shape=jax.ShapeDtypeStruct((m, d), F32),
        scratch_shapes=[pltpu.VMEM((tm, d), BF16), pltpu.VMEM((tm, d), F32)],
        compiler_params=_cparams(("parallel", "arbitrary")),
        name="mlp",
    )(h, g, wu, wd)


def _ple_kernel(h_ref, p_ref, g_ref, wg_ref, wp_ref, gf_ref, o_ref, *, final):
    h = h_ref[...]
    gate = jax.nn.sigmoid(_dot(_rms(h, g_ref[...]).astype(BF16), wg_ref[...]))
    out = h + _dot(p_ref[...].astype(BF16), wp_ref[...]) * gate
    if final:
        out = _rms(out, gf_ref[...])
    o_ref[...] = out


def ple(h, p, g, wg, wp, gf, final, tm=ROW_TILE):
    m, d = h.shape
    vec = pl.BlockSpec((1, d), lambda i: (0, 0))
    return pl.pallas_call(
        functools.partial(_ple_kernel, final=final),
        grid=(m // tm,),
        in_specs=[pl.BlockSpec((tm, d), lambda i: (i, 0)),
                  pl.BlockSpec((tm, PLE_DIM), lambda i: (i, 0)),
                  vec,
                  pl.BlockSpec((d, d), lambda i: (0, 0)),
                  pl.BlockSpec((PLE_DIM, d), lambda i: (0, 0)),
                  vec],
        out_specs=pl.BlockSpec((tm, d), lambda i: (i, 0)),
        out_shape=jax.ShapeDtypeStruct((m, d), F32),
        compiler_params=_cparams(("parallel",)),
        name="ple_final" if final else "ple",
    )(h, p, g, wg, wp, gf)


def _expand_matrix(direction):
    e = np.zeros((LANES, SSD_WIDTH), np.float32)
    for hd in range(SSD_HEADS):
        e[direction * SSD_HEADS + hd, hd * SSD_HEAD_DIM:(hd + 1) * SSD_HEAD_DIM] = 1.0
    return jnp.asarray(e)


def _pad_lanes(v):
    return jnp.pad(v.reshape(1, -1), ((0, 0), (0, LANES - v.size)))


def _layer_params(i, norm_mix_g, w_in, conv_w, conv_b, conv_norm_g, conv_norm_b, ssd_conv_w,
                  ssd_conv_b, ssd_dt_bias, ssd_a_log, ssd_d, ssd_norm_g, lambda_q1, lambda_k1,
                  lambda_q2, lambda_k2, attn_subln_g, w_out, norm_mlp_g, w_up, w_down,
                  norm_ple_g, w_ple, w_ple_gate):
    row = lambda v: v.reshape(1, -1)
    w = w_in[i]
    c_conv = 2 * CONV_CH
    c_dt = c_conv + SSD_WIDTH + SSD_XBC
    c_att = c_dt + 2 * SSD_HEADS
    w_ssd = jnp.pad(w[:, c_conv:c_att], ((0, 0), (0, SSD_DT_COLS - 2 * SSD_HEADS)))
    return dict(
        norm_mix_g=row(norm_mix_g[i]),
        w_conv=w[:, :c_conv].astype(BF16),
        w_ssd=w_ssd.astype(BF16),
        w_att=jnp.concatenate([w[:, c_att:c_att + ATT_WIDTH] * (LOG2E * ATT_HEAD_DIM ** -0.5),
                               w[:, c_att + ATT_WIDTH:]], axis=1).astype(BF16),
        conv_w=conv_w[i], conv_b=row(conv_b[i]), conv_g=row(conv_norm_g[i]),
        conv_beta=row(conv_norm_b[i]),
        ssd_cw=ssd_conv_w[i], ssd_cb=row(ssd_conv_b[i]),
        ssd_dtb=_pad_lanes(ssd_dt_bias[i]), ssd_alog=_pad_lanes(ssd_a_log[i]),
        ssd_dskip=row(jnp.repeat(ssd_d[i], SSD_HEAD_DIM)), ssd_ng=row(ssd_norm_g[i]),
        lam=jnp.stack([lambda_q1[i], lambda_k1[i], lambda_q2[i], lambda_k2[i]]),
        subln_g=row(attn_subln_g[i]),
        w_out=w_out[i].astype(BF16),
        norm_mlp_g=row(norm_mlp_g[i]),
        w_up=w_up[i].astype(BF16), w_down=w_down[i].astype(BF16),
        norm_ple_g=row(norm_ple_g[i]),
        w_ple=w_ple[i].astype(BF16), w_gate=w_ple_gate[i].astype(BF16),
    )


def _run(x, p, layers, bias, expand_f, expand_b, final_g):
    bsz, s, d = x.shape
    m = bsz * s
    h = x.reshape(m, d)
    for i, lp in enumerate(layers):
        lam_init = 0.8 - 0.6 * math.exp(-0.3 * i)
        pc = norm_proj(h, lp["norm_mix_g"], lp["w_conv"], F32).reshape(bsz, s, CONV_SLAB)
        ps = norm_proj(h, lp["norm_mix_g"], lp["w_ssd"], F32).reshape(bsz, s, SSD_SLAB)
        pa = norm_proj(h, lp["norm_mix_g"], lp["w_att"], BF16).reshape(bsz, s, ATT_SLAB)
        y_conv = conv_module(pc, lp["conv_w"], lp["conv_b"], lp["conv_g"], lp["conv_beta"])
        y_f = ssd_scan(ps, lp["ssd_cw"], lp["ssd_cb"], lp["ssd_dtb"], lp["ssd_alog"], expand_f,
                       reverse=False)
        y_ssd = ssd_scan(ps, lp["ssd_cw"], lp["ssd_cb"], lp["ssd_dtb"], lp["ssd_alog"], expand_b,
                         reverse=True, y_fwd=y_f, dskip=lp["ssd_dskip"], ng=lp["ssd_ng"])
        y_att = diff_attention(pa, bias, lp["lam"], lp["subln_g"], lam_init)
        h = out_proj(y_conv.reshape(m, CONV_CH), y_ssd.reshape(m, SSD_WIDTH),
                     y_att.reshape(m, ATT_WIDTH), h, lp["w_out"])
        h = mlp(h, lp["norm_mlp_g"], lp["w_up"], lp["w_down"])
        h = ple(h, p[i].reshape(m, PLE_DIM), lp["norm_ple_g"], lp["w_gate"], lp["w_ple"],
                final_g, final=(i == len(layers) - 1))
    return h.reshape(bsz, s, d)


def kernel(x_prompt, x_sample, p_prompt, p_sample, norm_mix_g, w_in, conv_w, conv_b, conv_norm_g,
           conv_norm_b, ssd_conv_w, ssd_conv_b, ssd_dt_bias, ssd_a_log, ssd_d, ssd_norm_g,
           lambda_q1, lambda_k1, lambda_q2, lambda_k2, attn_subln_g, rel_bias, w_out, norm_mlp_g,
           w_up, w_down, norm_ple_g, w_ple, w_ple_gate, final_norm_g):
    layers = [_layer_params(i, norm_mix_g, w_in, conv_w, conv_b, conv_norm_g, conv_norm_b,
                            ssd_conv_w, ssd_conv_b, ssd_dt_bias, ssd_a_log, ssd_d, ssd_norm_g,
                            lambda_q1, lambda_k1, lambda_q2, lambda_k2, attn_subln_g, w_out,
                            norm_mlp_g, w_up, w_down, norm_ple_g, w_ple, w_ple_gate)
              for i in range(DEPTH)]
    bias = bias_tiles(_bucket_tiles(ATT_TILE), rel_bias)
    expand_f, expand_b = _expand_matrix(0), _expand_matrix(1)
    final_g = final_norm_g.reshape(1, -1)
    y_prompt = _run(x_prompt, p_prompt, layers, bias, expand_f, expand_b, final_g)
    y_sample = _run(x_sample, p_sample, layers, bias, expand_f, expand_b, final_g)
    return (y_prompt, y_sample)
```

```python
import functools
import math

import jax
import jax.numpy as jnp
import numpy as np
from jax import lax
from jax.experimental import pallas as pl
from jax.experimental.pallas import tpu as pltpu

F32 = jnp.float32
BF16 = jnp.bfloat16

D_MODEL = 2048
DEPTH = 2
PLE_DIM = 256
EPS = 1e-6
CONV_CH = 512
CONV_WIDTH = 31
CONV_PAD = (CONV_WIDTH - 1) // 2
SSD_HEAD_DIM = 64
SSD_HEADS = 12
SSD_WIDTH = SSD_HEADS * SSD_HEAD_DIM
SSD_GROUPS = 2
SSD_HPG = SSD_HEADS // SSD_GROUPS
SSD_STATE = 128
SSD_CONV_WIDTH = 5
SSD_CONV_PAD = (SSD_CONV_WIDTH - 1) // 2
SSD_CHUNK = 128
SSD_XBC = SSD_WIDTH + 2 * SSD_GROUPS * SSD_STATE
ATT_HEADS = 6
ATT_HEAD_DIM = 64
ATT_WIDTH = ATT_HEADS * 2 * ATT_HEAD_DIM
REL_BUCKETS = 32
REL_MAX_DIST = 128
D_FF = 4 * D_MODEL
LOG2E = math.log2(math.e)

LANES = 128
SUBLANES = 8
VMEM_LIMIT = 56 * 1024 * 1024

SSD_DT_COLS = LANES
SSD_SLAB = SSD_WIDTH + SSD_XBC + SSD_DT_COLS
CONV_SLAB = 2 * CONV_CH
ATT_SLAB = 3 * ATT_WIDTH
SSD_GW = SSD_HPG * SSD_HEAD_DIM

ATT_TILE = 512
N_BIAS_TILES = 5
ATT_ROWS = 32
CONV_TILE = 128
CONV_HALO = 16
CONV_ROWS = 32
SSD_HALO = SUBLANES
ROW_TILE = 512
FF_TILE = 1024


def _cparams(sem):
    return pltpu.CompilerParams(dimension_semantics=sem, vmem_limit_bytes=VMEM_LIMIT)


def _rms(x, g):
    ms = jnp.mean(x * x, axis=-1, keepdims=True)
    return (x * lax.rsqrt(ms + EPS)) * g


def _silu(x):
    return x * jax.nn.sigmoid(x)


def _dot(a, b):
    return jnp.dot(a, b, preferred_element_type=F32)


def _split3(x):
    hi = x.astype(BF16)
    r1 = x - hi.astype(F32)
    mid = r1.astype(BF16)
    lo = (r1 - mid.astype(F32)).astype(BF16)
    return hi, mid, lo


def _select_dot(sel, x):
    sel = sel.astype(BF16)
    hi, mid, lo = _split3(x)
    return _dot(sel, hi) + (_dot(sel, mid) + _dot(sel, lo))


def _dot_select(x, sel):
    sel = sel.astype(BF16)
    hi, mid, lo = _split3(x)
    return _dot(hi, sel) + (_dot(mid, sel) + _dot(lo, sel))


def _dot_nt(a, b):
    return lax.dot_general(a, b, (((1,), (1,)), ((), ())), preferred_element_type=F32)


def _norm_proj_kernel(x_ref, g_ref, w_ref, o_ref):
    xn = _rms(x_ref[...], g_ref[...]).astype(BF16)
    o_ref[...] = _dot(xn, w_ref[...]).astype(o_ref.dtype)


def norm_proj(x, g, w, out_dtype, tm=ROW_TILE):
    m, d = x.shape
    n = w.shape[1]
    return pl.pallas_call(
        _norm_proj_kernel,
        grid=(m // tm,),
        in_specs=[pl.BlockSpec((tm, d), lambda i: (i, 0)),
                  pl.BlockSpec((1, d), lambda i: (0, 0)),
                  pl.BlockSpec((d, n), lambda i: (0, 0))],
        out_specs=pl.BlockSpec((tm, n), lambda i: (i, 0)),
        out_shape=jax.ShapeDtypeStruct((m, n), out_dtype),
        compiler_params=_cparams(("parallel",)),
        name="norm_proj",
    )(x, g, w)


def _conv_kernel(cur_ref, left_ref, right_ref, w_ref, b_ref, g_ref, beta_ref, o_ref, buf_ref,
                 shift_ref):
    t = pl.program_id(1)
    nt = pl.num_programs(1)

    def glu(blk):
        return blk[:, :CONV_CH] * jax.nn.sigmoid(blk[:, CONV_CH:])

    buf_ref[0:CONV_HALO, :] = jnp.where(t > 0, glu(left_ref[0]), 0.0)
    buf_ref[CONV_HALO:CONV_HALO + CONV_TILE, :] = glu(cur_ref[0])
    buf_ref[CONV_HALO + CONV_TILE:, :] = jnp.where(t < nt - 1, glu(right_ref[0]), 0.0)

    span = CONV_TILE + 2 * CONV_HALO - SUBLANES
    for b in range(1, SUBLANES):
        shift_ref[b - 1] = buf_ref[b:b + span, :]

    w = w_ref[...]
    for r0 in range(0, CONV_TILE, CONV_ROWS):
        acc = jnp.broadcast_to(b_ref[...], (CONV_ROWS, CONV_CH))
        for k in range(CONV_WIDTH):
            start = r0 + CONV_HALO - CONV_PAD + k
            b, base = start % SUBLANES, start - start % SUBLANES
            window = (buf_ref[base:base + CONV_ROWS, :] if b == 0
                      else shift_ref[b - 1, base:base + CONV_ROWS, :])
            acc = acc + w[k:k + 1, :] * window
        mu = jnp.mean(acc, axis=-1, keepdims=True)
        xc = acc - mu
        var = jnp.mean(xc * xc, axis=-1, keepdims=True)
        y = xc * lax.rsqrt(var + EPS) * g_ref[...] + beta_ref[...]
        o_ref[0, r0:r0 + CONV_ROWS, :] = _silu(y).astype(o_ref.dtype)


def conv_module(pc, w, b, g, beta):
    bsz, s, _ = pc.shape
    nt = s // CONV_TILE
    hb = CONV_TILE // CONV_HALO
    last_halo = s // CONV_HALO - 1
    vec = lambda: pl.BlockSpec((1, CONV_CH), lambda i, t: (0, 0))
    return pl.pallas_call(
        _conv_kernel,
        grid=(bsz, nt),
        in_specs=[
            pl.BlockSpec((1, CONV_TILE, CONV_SLAB), lambda i, t: (i, t, 0)),
            pl.BlockSpec((1, CONV_HALO, CONV_SLAB), lambda i, t: (i, jnp.maximum(t * hb - 1, 0), 0)),
            pl.BlockSpec((1, CONV_HALO, CONV_SLAB),
                         lambda i, t: (i, jnp.minimum((t + 1) * hb, last_halo), 0)),
            pl.BlockSpec((CONV_WIDTH, CONV_CH), lambda i, t: (0, 0)),
            vec(), vec(), vec(),
        ],
        out_specs=pl.BlockSpec((1, CONV_TILE, CONV_CH), lambda i, t: (i, t, 0)),
        out_shape=jax.ShapeDtypeStruct((bsz, s, CONV_CH), BF16),
        scratch_shapes=[pltpu.VMEM((CONV_TILE + 2 * CONV_HALO, CONV_CH), F32),
                        pltpu.VMEM((SUBLANES - 1, CONV_TILE + 2 * CONV_HALO - SUBLANES, CONV_CH),
                                   F32)],
        compiler_params=_cparams(("parallel", "parallel")),
        name="conv_module",
    )(pc, pc, pc, w, b, g, beta)


def _ssd_kernel(*refs, reverse):
    if reverse:
        (cur_ref, left_ref, right_ref, cw_ref, cb_ref, dtb_ref, alog_ref, exp_ref,
         yf_ref, dskip_ref, ng_ref, o_ref, state_ref, buf_ref) = refs
    else:
        (cur_ref, left_ref, right_ref, cw_ref, cb_ref, dtb_ref, alog_ref, exp_ref,
         o_ref, state_ref, buf_ref) = refs
    q = SSD_CHUNK
    c = pl.program_id(1)
    nc = pl.num_programs(1)
    tc = (nc - 1 - c) if reverse else c

    @pl.when(c == 0)
    def _():
        state_ref[...] = jnp.zeros_like(state_ref)

    blk = cur_ref[0]
    xbc_lo, xbc_hi = SSD_WIDTH, SSD_WIDTH + SSD_XBC
    buf_ref[0:SSD_HALO, :] = jnp.where(tc > 0, left_ref[0][:, xbc_lo:xbc_hi], 0.0)
    buf_ref[SSD_HALO:SSD_HALO + q, :] = blk[:, xbc_lo:xbc_hi]
    buf_ref[SSD_HALO + q:, :] = jnp.where(tc < nc - 1, right_ref[0][:, xbc_lo:xbc_hi], 0.0)
    acc = jnp.broadcast_to(cb_ref[...], (q, SSD_XBC))
    cw = cw_ref[...]
    for k in range(SSD_CONV_WIDTH):
        start = SSD_HALO - SSD_CONV_PAD + k
        acc = acc + cw[k:k + 1, :] * buf_ref[start:start + q, :]
    xbc = _silu(acc)
    xs = xbc[:, :SSD_WIDTH]
    bmat = xbc[:, SSD_WIDTH:SSD_WIDTH + SSD_GROUPS * SSD_STATE]
    cmat = xbc[:, SSD_WIDTH + SSD_GROUPS * SSD_STATE:]

    dt = jax.nn.softplus(blk[:, xbc_hi:] + dtb_ref[...])
    dta = dt * (-jnp.exp(alog_ref[...]))
    row = lax.broadcasted_iota(jnp.int32, (q, q), 0)
    col = lax.broadcasted_iota(jnp.int32, (q, q), 1)
    keep = (row <= col) if reverse else (row >= col)
    a_cs = _select_dot(keep, dta)
    a_tot = jnp.sum(dta, axis=0, keepdims=True)
    ea = jnp.exp(a_cs)
    dec = jnp.exp(a_tot - a_cs)
    cdec = jnp.broadcast_to(jnp.exp(a_tot), (SUBLANES, LANES))
    ex = _dot_select(jnp.concatenate([dt, ea, dec, cdec], axis=0), exp_ref[...])
    dtx, eax, decx, cdecx = ex[0:q], ex[q:2 * q], ex[2 * q:3 * q], ex[3 * q:3 * q + 1]
    xdt = xs * dtx
    xdt_b = xdt.astype(BF16)
    xd_b = (xdt * decx).astype(BF16)
    a_cs_t = a_cs.T
    lane = lax.broadcasted_iota(jnp.int32, (q, LANES), 1)
    lo_half = lane < SSD_HEAD_DIM
    lane0 = SSD_HEADS if reverse else 0

    y_groups = []
    for g in range(SSD_GROUPS):
        gs = slice(g * SSD_GW, (g + 1) * SSD_GW)
        bg = bmat[:, g * SSD_STATE:(g + 1) * SSD_STATE]
        cg_b = cmat[:, g * SSD_STATE:(g + 1) * SSD_STATE].astype(BF16)
        cb = _dot_nt(cg_b, bg.astype(BF16))
        prev = state_ref[:, gs]
        y_off = _dot(cg_b, prev.astype(BF16)) * eax[:, gs]
        new_states = _dot(bg.T.astype(BF16), xd_b[:, gs])
        state_ref[:, gs] = prev * cdecx[:, gs] + new_states
        pairs = []
        for j in range(SSD_HPG // 2):
            xpair = xdt_b[:, g * SSD_GW + j * LANES:g * SSD_GW + (j + 1) * LANES]
            ypair = None
            for half in range(2):
                ln = lane0 + g * SSD_HPG + 2 * j + half
                seg = a_cs[:, ln:ln + 1] - a_cs_t[ln:ln + 1, :]
                decay = jnp.exp(jnp.where(keep, seg, -jnp.inf))
                mixer = (cb * decay).astype(BF16)
                xh = jnp.where(lo_half if half == 0 else jnp.logical_not(lo_half), xpair,
                               jnp.zeros_like(xpair))
                part = _dot(mixer, xh)
                ypair = part if ypair is None else ypair + part
            pairs.append(ypair)
        y_groups.append(jnp.concatenate(pairs, axis=1) + y_off)
    y = jnp.concatenate(y_groups, axis=1)

    if reverse:
        z = blk[:, :SSD_WIDTH]
        y = y + yf_ref[0] + dskip_ref[...] * xs
        o_ref[0] = _rms(y * _silu(z), ng_ref[...]).astype(o_ref.dtype)
    else:
        o_ref[0] = y


def ssd_scan(ps, cw, cb, dtb, alog, expand, *, reverse, y_fwd=None, dskip=None, ng=None):
    bsz, s, _ = ps.shape
    q = SSD_CHUNK
    nc = s // q
    hb = q // SSD_HALO
    last_halo = s // SSD_HALO - 1
    tpos = (lambda c: nc - 1 - c) if reverse else (lambda c: c)
    full = lambda shape: pl.BlockSpec(shape, lambda i, c: (0,) * len(shape))
    in_specs = [
        pl.BlockSpec((1, q, SSD_SLAB), lambda i, c: (i, tpos(c), 0)),
        pl.BlockSpec((1, SSD_HALO, SSD_SLAB), lambda i, c: (i, jnp.maximum(tpos(c) * hb - 1, 0), 0)),
        pl.BlockSpec((1, SSD_HALO, SSD_SLAB),
                     lambda i, c: (i, jnp.minimum((tpos(c) + 1) * hb, last_halo), 0)),
        full((SSD_CONV_WIDTH, SSD_XBC)), full((1, SSD_XBC)), full((1, LANES)), full((1, LANES)),
        full((LANES, SSD_WIDTH)),
    ]
    args = [ps, ps, ps, cw, cb, dtb, alog, expand]
    if reverse:
        in_specs += [pl.BlockSpec((1, q, SSD_WIDTH), lambda i, c: (i, tpos(c), 0)),
                     full((1, SSD_WIDTH)), full((1, SSD_WIDTH))]
        args += [y_fwd, dskip, ng]
    return pl.pallas_call(
        functools.partial(_ssd_kernel, reverse=reverse),
        grid=(bsz, nc),
        in_specs=in_specs,
        out_specs=pl.BlockSpec((1, q, SSD_WIDTH), lambda i, c: (i, tpos(c), 0)),
        out_shape=jax.ShapeDtypeStruct((bsz, s, SSD_WIDTH), BF16 if reverse else F32),
        scratch_shapes=[pltpu.VMEM((SSD_STATE, SSD_WIDTH), F32),
                        pltpu.VMEM((q + 2 * SSD_HALO, SSD_XBC), F32)],
        compiler_params=_cparams(("parallel", "arbitrary")),
        name="ssd_bwd" if reverse else "ssd_fwd",
    )(*args)


def _bias_tiles_kernel(bucket_ref, table_ref, o_ref):
    h = pl.program_id(0)
    bucket = bucket_ref[0]
    out = jnp.zeros(bucket.shape, F32)
    for b in range(REL_BUCKETS):
        out = jnp.where(bucket == b, table_ref[b, h] * LOG2E, out)
    o_ref[0, 0] = out


def bias_tiles(buckets, rel_bias):
    nt, t, _ = buckets.shape
    return pl.pallas_call(
        _bias_tiles_kernel,
        grid=(ATT_HEADS, nt),
        in_specs=[pl.BlockSpec((1, t, t), lambda h, d: (d, 0, 0)),
                  pl.BlockSpec(memory_space=pltpu.SMEM)],
        out_specs=pl.BlockSpec((1, 1, t, t), lambda h, d: (h, d, 0, 0)),
        out_shape=jax.ShapeDtypeStruct((ATT_HEADS, nt, t, t), F32),
        compiler_params=_cparams(("parallel", "parallel")),
        name="bias_tiles",
    )(buckets, rel_bias)


def _rel_bucket(rel):
    nb = REL_BUCKETS // 2
    max_exact = nb // 2
    ret = jnp.where(rel > 0, nb, 0)
    n = jnp.abs(rel)
    nf = jnp.maximum(n, 1).astype(jnp.float32)
    large = max_exact + (jnp.log(nf / max_exact) / math.log(REL_MAX_DIST / max_exact)
                         * (nb - max_exact)).astype(jnp.int32)
    large = jnp.minimum(large, nb - 1)
    return ret + jnp.where(n < max_exact, n, large)


def _bucket_tiles(t):
    i = jnp.arange(t, dtype=jnp.int32)
    half = N_BIAS_TILES // 2
    offs = (jnp.arange(N_BIAS_TILES, dtype=jnp.int32) - half) * t
    rel = offs[:, None, None] + i[None, None, :] - i[None, :, None]
    return _rel_bucket(rel).astype(jnp.int32)


def _attn_kernel(q_ref, k_ref, v_ref, bias_ref, lam_ref, g_ref, o_ref,
                 qz_ref, s_ref, p_ref, m_ref, l_ref, alpha_ref, acc_ref, *, lam_init, nk):
    t = ATT_TILE
    qi = pl.program_id(2)
    qb = q_ref[0]
    lane = lax.broadcasted_iota(jnp.int32, (t, 2 * ATT_HEAD_DIM), 1)
    qz_ref[0] = jnp.where(lane < ATT_HEAD_DIM, qb, jnp.zeros_like(qb))
    qz_ref[1] = jnp.where(lane >= ATT_HEAD_DIM, qb, jnp.zeros_like(qb))
    m_ref[...] = jnp.full(m_ref.shape, -jnp.inf, F32)
    l_ref[...] = jnp.zeros_like(l_ref)
    acc_ref[...] = jnp.zeros_like(acc_ref)
    half = N_BIAS_TILES // 2
    rb = ATT_ROWS

    def key_rows(kj):
        start = kj * t
        return pl.ds(start if isinstance(start, int) else pl.multiple_of(start, t), t)

    def apply_pv(slot, kj):
        vb = v_ref[0, key_rows(kj), :]
        for mp in range(2):
            acc_ref[mp] = alpha_ref[slot, mp] * acc_ref[mp] + _dot(p_ref[slot, mp], vb)

    def pair(kk, carry, first=False):
        for slot in range(2):
            kj = 2 * kk + slot
            kb = k_ref[0, key_rows(kj), :]
            tile = jnp.clip(kj - qi, -half, half) + half
            for mp in range(2):
                s_ref[slot, mp] = _dot_nt(qz_ref[mp], kb)
            if not (first and slot == 0):
                apply_pv(1 - slot, kj - 1)
            for mp in range(2):
                for r0 in range(0, t, rb):
                    rows = slice(r0, r0 + rb)
                    s = s_ref[slot, mp, rows, :] + bias_ref[0, tile, rows, :]
                    m_old = m_ref[mp, rows, :]
                    m_new = jnp.maximum(m_old, jnp.max(s, axis=-1, keepdims=True))
                    alpha = jnp.exp2(m_old - m_new)
                    p = jnp.exp2(s - jnp.concatenate([m_new] * (t // LANES), axis=1))
                    l_ref[mp, rows, :] = (alpha * l_ref[mp, rows, :]
                                          + jnp.sum(p, axis=-1, keepdims=True))
                    m_ref[mp, rows, :] = m_new
                    alpha_ref[slot, mp, rows, :] = alpha
                    p_ref[slot, mp, rows, :] = p.astype(BF16)
        return carry

    pair(0, 0, first=True)
    lax.fori_loop(1, nk // 2, pair, 0)
    apply_pv(1, nk - 1)

    lp = lam_ref[...]
    lam = (jnp.exp(jnp.sum(lp[0:1] * lp[1:2], axis=-1, keepdims=True))
           - jnp.exp(jnp.sum(lp[2:3] * lp[3:4], axis=-1, keepdims=True)) + lam_init)
    o = acc_ref[0] / l_ref[0] - lam * (acc_ref[1] / l_ref[1])
    o_ref[0] = (_rms(o, g_ref[...]) * (1.0 - lam_init)).astype(o_ref.dtype)


def diff_attention(qkv, bias, lam_params, subln_g, lam_init):
    bsz, s, _ = qkv.shape
    t = ATT_TILE
    nq = s // t
    hw = 2 * ATT_HEAD_DIM
    assert nq % 2 == 0
    stat = pltpu.VMEM((2, t, LANES), F32)
    return pl.pallas_call(
        functools.partial(_attn_kernel, lam_init=lam_init, nk=nq),
        grid=(bsz, ATT_HEADS, nq),
        in_specs=[
            pl.BlockSpec((1, t, hw), lambda b, h, i: (b, i, h)),
            pl.BlockSpec((1, s, hw), lambda b, h, i: (b, 0, ATT_HEADS + h)),
            pl.BlockSpec((1, s, hw), lambda b, h, i: (b, 0, 2 * ATT_HEADS + h)),
            pl.BlockSpec((1, N_BIAS_TILES, t, t), lambda b, h, i: (h, 0, 0, 0)),
            pl.BlockSpec((4, ATT_HEAD_DIM), lambda b, h, i: (0, 0)),
            pl.BlockSpec((1, hw), lambda b, h, i: (0, 0)),
        ],
        out_specs=pl.BlockSpec((1, t, hw), lambda b, h, i: (b, i, h)),
        out_shape=jax.ShapeDtypeStruct((bsz, s, ATT_WIDTH), BF16),
        scratch_shapes=[pltpu.VMEM((2, t, hw), BF16),
                        pltpu.VMEM((2, 2, t, t), F32),
                        pltpu.VMEM((2, 2, t, t), BF16),
                        stat, stat,
                        pltpu.VMEM((2, 2, t, LANES), F32),
                        pltpu.VMEM((2, t, hw), F32)],
        compiler_params=_cparams(("parallel", "parallel", "parallel")),
        name="diff_attention",
    )(qkv, qkv, qkv, bias, lam_params, subln_g)


def _out_proj_kernel(yc_ref, ys_ref, ya_ref, h_ref, w_ref, o_ref):
    c0, c1 = CONV_CH, CONV_CH + SSD_WIDTH
    acc = _dot(yc_ref[...], w_ref[0:c0, :])
    acc = acc + _dot(ys_ref[...], w_ref[c0:c1, :])
    acc = acc + _dot(ya_ref[...], w_ref[c1:, :])
    o_ref[...] = h_ref[...] + acc


def out_proj(yc, ys, ya, h, w, tm=ROW_TILE):
    m, d = h.shape
    row = lambda n: pl.BlockSpec((tm, n), lambda i: (i, 0))
    return pl.pallas_call(
        _out_proj_kernel,
        grid=(m // tm,),
        in_specs=[row(CONV_CH), row(SSD_WIDTH), row(ATT_WIDTH), row(d),
                  pl.BlockSpec(w.shape, lambda i: (0, 0))],
        out_specs=row(d),
        out_shape=jax.ShapeDtypeStruct((m, d), F32),
        compiler_params=_cparams(("parallel",)),
        name="out_proj",
    )(yc, ys, ya, h, w)


def _mlp_kernel(h_ref, g_ref, wu_ref, wd_ref, o_ref, xn_ref, acc_ref):
    f = pl.program_id(1)

    @pl.when(f == 0)
    def _():
        xn_ref[...] = _rms(h_ref[...], g_ref[...]).astype(BF16)
        acc_ref[...] = jnp.zeros_like(acc_ref)

    hid = jnp.square(jnp.maximum(_dot(xn_ref[...], wu_ref[...]), 0.0)).astype(BF16)
    acc_ref[...] += _dot(hid, wd_ref[...])

    @pl.when(f == pl.num_programs(1) - 1)
    def _():
        o_ref[...] = h_ref[...] + acc_ref[...]


def mlp(h, g, wu, wd, tm=ROW_TILE, tf=FF_TILE):
    m, d = h.shape
    ff = wu.shape[1]
    return pl.pallas_call(
        _mlp_kernel,
        grid=(m // tm, ff // tf),
        in_specs=[pl.BlockSpec((tm, d), lambda i, f: (i, 0)),
                  pl.BlockSpec((1, d), lambda i, f: (0, 0)),
                  pl.BlockSpec((d, tf), lambda i, f: (0, f)),
                  pl.BlockSpec((tf, d), lambda i, f: (f, 0))],
        out_specs=pl.BlockSpec((tm, d), lambda i, f: (i, 0)),
        out_shape=jax.ShapeDtypeStruct((m, d), F32),
        scratch_shapes=[pltpu.VMEM((tm, d), BF16), pltpu.VMEM((tm, d), F32)],
        compiler_params=_cparams(("parallel", "arbitrary")),
        name="mlp",
    )(h, g, wu, wd)


def _ple_kernel(h_ref, p_ref, g_ref, wg_ref, wp_ref, gf_ref, o_ref, *, final):
    h = h_ref[...]
    gate = jax.nn.sigmoid(_dot(_rms(h, g_ref[...]).astype(BF16), wg_ref[...]))
    out = h + _dot(p_ref[...].astype(BF16), wp_ref[...]) * gate
    if final:
        out = _rms(out, gf_ref[...])
    o_ref[...] = out


def ple(h, p, g, wg, wp, gf, final, tm=ROW_TILE):
    m, d = h.shape
    vec = pl.BlockSpec((1, d), lambda i: (0, 0))
    return pl.pallas_call(
        functools.partial(_ple_kernel, final=final),
        grid=(m // tm,),
        in_specs=[pl.BlockSpec((tm, d), lambda i: (i, 0)),
                  pl.BlockSpec((tm, PLE_DIM), lambda i: (i, 0)),
                  vec,
                  pl.BlockSpec((d, d), lambda i: (0, 0)),
                  pl.BlockSpec((PLE_DIM, d), lambda i: (0, 0)),
                  vec],
        out_specs=pl.BlockSpec((tm, d), lambda i: (i, 0)),
        out_shape=jax.ShapeDtypeStruct((m, d), F32),
        compiler_params=_cparams(("parallel",)),
        name="ple_final" if final else "ple",
    )(h, p, g, wg, wp, gf)


def _expand_matrix(direction):
    e = np.zeros((LANES, SSD_WIDTH), np.float32)
    for hd in range(SSD_HEADS):
        e[direction * SSD_HEADS + hd, hd * SSD_HEAD_DIM:(hd + 1) * SSD_HEAD_DIM] = 1.0
    return jnp.asarray(e)


def _pad_lanes(v):
    return jnp.pad(v.reshape(1, -1), ((0, 0), (0, LANES - v.size)))


def _layer_params(i, norm_mix_g, w_in, conv_w, conv_b, conv_norm_g, conv_norm_b, ssd_conv_w,
                  ssd_conv_b, ssd_dt_bias, ssd_a_log, ssd_d, ssd_norm_g, lambda_q1, lambda_k1,
                  lambda_q2, lambda_k2, attn_subln_g, w_out, norm_mlp_g, w_up, w_down,
                  norm_ple_g, w_ple, w_ple_gate):
    row = lambda v: v.reshape(1, -1)
    w = w_in[i]
    c_conv = 2 * CONV_CH
    c_dt = c_conv + SSD_WIDTH + SSD_XBC
    c_att = c_dt + 2 * SSD_HEADS
    w_ssd = jnp.pad(w[:, c_conv:c_att], ((0, 0), (0, SSD_DT_COLS - 2 * SSD_HEADS)))
    return dict(
        norm_mix_g=row(norm_mix_g[i]),
        w_conv=w[:, :c_conv].astype(BF16),
        w_ssd=w_ssd.astype(BF16),
        w_att=jnp.concatenate([w[:, c_att:c_att + ATT_WIDTH] * (LOG2E * ATT_HEAD_DIM ** -0.5),
                               w[:, c_att + ATT_WIDTH:]], axis=1).astype(BF16),
        conv_w=conv_w[i], conv_b=row(conv_b[i]), conv_g=row(conv_norm_g[i]),
        conv_beta=row(conv_norm_b[i]),
        ssd_cw=ssd_conv_w[i], ssd_cb=row(ssd_conv_b[i]),
        ssd_dtb=_pad_lanes(ssd_dt_bias[i]), ssd_alog=_pad_lanes(ssd_a_log[i]),
        ssd_dskip=row(jnp.repeat(ssd_d[i], SSD_HEAD_DIM)), ssd_ng=row(ssd_norm_g[i]),
        lam=jnp.stack([lambda_q1[i], lambda_k1[i], lambda_q2[i], lambda_k2[i]]),
        subln_g=row(attn_subln_g[i]),
        w_out=w_out[i].astype(BF16),
        norm_mlp_g=row(norm_mlp_g[i]),
        w_up=w_up[i].astype(BF16), w_down=w_down[i].astype(BF16),
        norm_ple_g=row(norm_ple_g[i]),
        w_ple=w_ple[i].astype(BF16), w_gate=w_ple_gate[i].astype(BF16),
    )


def _run(x, p, layers, bias, expand_f, expand_b, final_g):
    bsz, s, d = x.shape
    m = bsz * s
    h = x.reshape(m, d)
    for i, lp in enumerate(layers):
        lam_init = 0.8 - 0.6 * math.exp(-0.3 * i)
        pc = norm_proj(h, lp["norm_mix_g"], lp["w_conv"], F32).reshape(bsz, s, CONV_SLAB)
        ps = norm_proj(h, lp["norm_mix_g"], lp["w_ssd"], F32).reshape(bsz, s, SSD_SLAB)
        pa = norm_proj(h, lp["norm_mix_g"], lp["w_att"], BF16).reshape(bsz, s, ATT_SLAB)
        y_conv = conv_module(pc, lp["conv_w"], lp["conv_b"], lp["conv_g"], lp["conv_beta"])
        y_f = ssd_scan(ps, lp["ssd_cw"], lp["ssd_cb"], lp["ssd_dtb"], lp["ssd_alog"], expand_f,
                       reverse=False)
        y_ssd = ssd_scan(ps, lp["ssd_cw"], lp["ssd_cb"], lp["ssd_dtb"], lp["ssd_alog"], expand_b,
                         reverse=True, y_fwd=y_f, dskip=lp["ssd_dskip"], ng=lp["ssd_ng"])
        y_att = diff_attention(pa, bias, lp["lam"], lp["subln_g"], lam_init)
        h = out_proj(y_conv.reshape(m, CONV_CH), y_ssd.reshape(m, SSD_WIDTH),
                     y_att.reshape(m, ATT_WIDTH), h, lp["w_out"])
        h = mlp(h, lp["norm_mlp_g"], lp["w_up"], lp["w_down"])
        h = ple(h, p[i].reshape(m, PLE_DIM), lp["norm_ple_g"], lp["w_gate"], lp["w_ple"],
                final_g, final=(i == len(layers) - 1))
    return h.reshape(bsz, s, d)


def kernel(x_prompt, x_sample, p_prompt, p_sample, norm_mix_g, w_in, conv_w, conv_b, conv_norm_g,
           conv_norm_b, ssd_conv_w, ssd_conv_b, ssd_dt_bias, ssd_a_log, ssd_d, ssd_norm_g,
           lambda_q1, lambda_k1, lambda_q2, lambda_k2, attn_subln_g, rel_bias, w_out, norm_mlp_g,
           w_up, w_down, norm_ple_g, w_ple, w_ple_gate, final_norm_g):
    layers = [_layer_params(i, norm_mix_g, w_in, conv_w, conv_b, conv_norm_g, conv_norm_b,
                            ssd_conv_w, ssd_conv_b, ssd_dt_bias, ssd_a_log, ssd_d, ssd_norm_g,
                            lambda_q1, lambda_k1, lambda_q2, lambda_k2, attn_subln_g, w_out,
                            norm_mlp_g, w_up, w_down, norm_ple_g, w_ple, w_ple_gate)
              for i in range(DEPTH)]
    bias = bias_tiles(_bucket_tiles(ATT_TILE), rel_bias)
    expand_f, expand_b = _expand_matrix(0), _expand_matrix(1)
    final_g = final_norm_g.reshape(1, -1)
    y_prompt = _run(x_prompt, p_prompt, layers, bias, expand_f, expand_b, final_g)
    y_sample = _run(x_sample, p_sample, layers, bias, expand_f, expand_b, final_g)
    return (y_prompt, y_sample)
```

```python
import functools
import math

import jax
import jax.numpy as jnp
import numpy as np
from jax import lax
from jax.experimental import pallas as pl
from jax.experimental.pallas import tpu as pltpu

F32 = jnp.float32
BF16 = jnp.bfloat16

D_MODEL = 2048
DEPTH = 2
PLE_DIM = 256
EPS = 1e-6
CONV_CH = 512
CONV_WIDTH = 31
CONV_PAD = (CONV_WIDTH - 1) // 2
SSD_HEAD_DIM = 64
SSD_HEADS = 12
SSD_WIDTH = SSD_HEADS * SSD_HEAD_DIM
SSD_GROUPS = 2
SSD_HPG = SSD_HEADS // SSD_GROUPS
SSD_STATE = 128
SSD_CONV_WIDTH = 5
SSD_CONV_PAD = (SSD_CONV_WIDTH - 1) // 2
SSD_CHUNK = 128
SSD_XBC = SSD_WIDTH + 2 * SSD_GROUPS * SSD_STATE
ATT_HEADS = 6
ATT_HEAD_DIM = 64
ATT_WIDTH = ATT_HEADS * 2 * ATT_HEAD_DIM
REL_BUCKETS = 32
REL_MAX_DIST = 128
D_FF = 4 * D_MODEL
LOG2E = math.log2(math.e)

LANES = 128
SUBLANES = 8
VMEM_LIMIT = 56 * 1024 * 1024

SSD_DT_COLS = LANES
SSD_SLAB = SSD_WIDTH + SSD_XBC + SSD_DT_COLS
CONV_SLAB = 2 * CONV_CH
ATT_SLAB = 3 * ATT_WIDTH
SSD_GW = SSD_HPG * SSD_HEAD_DIM

ATT_TILE = 512
ATT_QTILES = 2
N_BIAS_TILES = 5
ATT_ROWS = 32
CONV_TILE = 128
CONV_HALO = 16
CONV_ROWS = 32
SSD_HALO = SUBLANES
SSD_BATCH = 2
ROW_TILE = 512
FF_TILE = 1024


def _cparams(sem):
    return pltpu.CompilerParams(dimension_semantics=sem, vmem_limit_bytes=VMEM_LIMIT)


def _rms(x, g):
    ms = jnp.mean(x * x, axis=-1, keepdims=True)
    return (x * lax.rsqrt(ms + EPS)) * g


def _silu(x):
    return x * jax.nn.sigmoid(x)


def _dot(a, b):
    return jnp.dot(a, b, preferred_element_type=F32)


def _split3(x):
    hi = x.astype(BF16)
    r1 = x - hi.astype(F32)
    mid = r1.astype(BF16)
    lo = (r1 - mid.astype(F32)).astype(BF16)
    return hi, mid, lo


def _select_dot(sel, x):
    sel = sel.astype(BF16)
    hi, mid, lo = _split3(x)
    return _dot(sel, hi) + (_dot(sel, mid) + _dot(sel, lo))


def _dot_select(x, sel):
    sel = sel.astype(BF16)
    hi = x.astype(BF16)
    mid = (x - hi.astype(F32)).astype(BF16)
    return _dot(hi, sel) + _dot(mid, sel)


def _dot_nt(a, b):
    return lax.dot_general(a, b, (((1,), (1,)), ((), ())), preferred_element_type=F32)


def _norm_proj_kernel(x_ref, g_ref, wc_ref, ws_ref, wa_ref, oc_ref, os_ref, oa_ref):
    xn = _rms(x_ref[...], g_ref[...]).astype(BF16)
    oc_ref[...] = _dot(xn, wc_ref[...])
    os_ref[...] = _dot(xn, ws_ref[...])
    oa_ref[...] = _dot(xn, wa_ref[...]).astype(oa_ref.dtype)


def norm_proj(x, g, w_conv, w_ssd, w_att, tm=ROW_TILE):
    m, d = x.shape
    resident = lambda w: pl.BlockSpec(w.shape, lambda i: (0, 0), pipeline_mode=pl.Buffered(1))
    rows = lambda n: pl.BlockSpec((tm, n), lambda i: (i, 0))
    return pl.pallas_call(
        _norm_proj_kernel,
        grid=(m // tm,),
        in_specs=[rows(d), pl.BlockSpec((1, d), lambda i: (0, 0)),
                  resident(w_conv), resident(w_ssd), resident(w_att)],
        out_specs=[rows(CONV_SLAB), rows(SSD_SLAB), rows(ATT_SLAB)],
        out_shape=[jax.ShapeDtypeStruct((m, CONV_SLAB), F32),
                   jax.ShapeDtypeStruct((m, SSD_SLAB), F32),
                   jax.ShapeDtypeStruct((m, ATT_SLAB), BF16)],
        compiler_params=_cparams(("parallel",)),
        name="norm_proj",
    )(x, g, w_conv, w_ssd, w_att)


def _conv_kernel(cur_ref, left_ref, right_ref, w_ref, b_ref, g_ref, beta_ref, o_ref, buf_ref,
                 shift_ref):
    t = pl.program_id(1)
    nt = pl.num_programs(1)

    def glu(blk):
        return blk[:, :CONV_CH] * jax.nn.sigmoid(blk[:, CONV_CH:])

    buf_ref[0:CONV_HALO, :] = jnp.where(t > 0, glu(left_ref[0]), 0.0)
    buf_ref[CONV_HALO:CONV_HALO + CONV_TILE, :] = glu(cur_ref[0])
    buf_ref[CONV_HALO + CONV_TILE:, :] = jnp.where(t < nt - 1, glu(right_ref[0]), 0.0)

    span = CONV_TILE + 2 * CONV_HALO - SUBLANES
    for b in range(1, SUBLANES):
        shift_ref[b - 1] = buf_ref[b:b + span, :]

    w = w_ref[...]
    for r0 in range(0, CONV_TILE, CONV_ROWS):
        acc = jnp.broadcast_to(b_ref[...], (CONV_ROWS, CONV_CH))
        for k in range(CONV_WIDTH):
            start = r0 + CONV_HALO - CONV_PAD + k
            b, base = start % SUBLANES, start - start % SUBLANES
            window = (buf_ref[base:base + CONV_ROWS, :] if b == 0
                      else shift_ref[b - 1, base:base + CONV_ROWS, :])
            acc = acc + w[k:k + 1, :] * window
        mu = jnp.mean(acc, axis=-1, keepdims=True)
        xc = acc - mu
        var = jnp.mean(xc * xc, axis=-1, keepdims=True)
        y = xc * lax.rsqrt(var + EPS) * g_ref[...] + beta_ref[...]
        o_ref[0, r0:r0 + CONV_ROWS, :] = _silu(y).astype(o_ref.dtype)


def conv_module(pc, w, b, g, beta):
    bsz, s, _ = pc.shape
    nt = s // CONV_TILE
    hb = CONV_TILE // CONV_HALO
    last_halo = s // CONV_HALO - 1
    vec = lambda: pl.BlockSpec((1, CONV_CH), lambda i, t: (0, 0))
    return pl.pallas_call(
        _conv_kernel,
        grid=(bsz, nt),
        in_specs=[
            pl.BlockSpec((1, CONV_TILE, CONV_SLAB), lambda i, t: (i, t, 0)),
            pl.BlockSpec((1, CONV_HALO, CONV_SLAB), lambda i, t: (i, jnp.maximum(t * hb - 1, 0), 0)),
            pl.BlockSpec((1, CONV_HALO, CONV_SLAB),
                         lambda i, t: (i, jnp.minimum((t + 1) * hb, last_halo), 0)),
            pl.BlockSpec((CONV_WIDTH, CONV_CH), lambda i, t: (0, 0)),
            vec(), vec(), vec(),
        ],
        out_specs=pl.BlockSpec((1, CONV_TILE, CONV_CH), lambda i, t: (i, t, 0)),
        out_shape=jax.ShapeDtypeStruct((bsz, s, CONV_CH), BF16),
        scratch_shapes=[pltpu.VMEM((CONV_TILE + 2 * CONV_HALO, CONV_CH), F32),
                        pltpu.VMEM((SUBLANES - 1, CONV_TILE + 2 * CONV_HALO - SUBLANES, CONV_CH),
                                   F32)],
        compiler_params=_cparams(("parallel", "parallel")),
        name="conv_module",
    )(pc, pc, pc, w, b, g, beta)


def _ssd_chunk(xbc, dt_raw, dtb, alog, expand, state_ref, bi, *, reverse):
    q = SSD_CHUNK
    xs = xbc[:, :SSD_WIDTH]
    bmat = xbc[:, SSD_WIDTH:SSD_WIDTH + SSD_GROUPS * SSD_STATE]
    cmat = xbc[:, SSD_WIDTH + SSD_GROUPS * SSD_STATE:]

    dt = jax.nn.softplus(dt_raw + dtb)
    dta = dt * (-jnp.exp(alog))
    row = lax.broadcasted_iota(jnp.int32, (q, q), 0)
    col = lax.broadcasted_iota(jnp.int32, (q, q), 1)
    keep = (row <= col) if reverse else (row >= col)
    a_cs = _select_dot(keep, dta)
    a_tot = jnp.sum(dta, axis=0, keepdims=True)
    ea = jnp.exp(a_cs)
    dec = jnp.exp(a_tot - a_cs)
    cdec = jnp.broadcast_to(jnp.exp(a_tot), (SUBLANES, LANES))
    ex = _dot_select(jnp.concatenate([dt, ea, dec, cdec], axis=0), expand)
    dtx, eax, decx, cdecx = ex[0:q], ex[q:2 * q], ex[2 * q:3 * q], ex[3 * q:3 * q + 1]
    xdt = xs * dtx
    xdt_b = xdt.astype(BF16)
    xd_b = (xdt * decx).astype(BF16)
    a_cs_t = a_cs.T
    lane = lax.broadcasted_iota(jnp.int32, (q, LANES), 1)
    lo_half = lane < SSD_HEAD_DIM
    lane0 = SSD_HEADS if reverse else 0

    y_groups = []
    for g in range(SSD_GROUPS):
        gs = slice(g * SSD_GW, (g + 1) * SSD_GW)
        bg = bmat[:, g * SSD_STATE:(g + 1) * SSD_STATE]
        cg_b = cmat[:, g * SSD_STATE:(g + 1) * SSD_STATE].astype(BF16)
        cb = _dot_nt(cg_b, bg.astype(BF16))
        prev = state_ref[bi, :, gs]
        y_off = _dot(cg_b, prev.astype(BF16)) * eax[:, gs]
        new_states = _dot(bg.T.astype(BF16), xd_b[:, gs])
        state_ref[bi, :, gs] = prev * cdecx[:, gs] + new_states
        pairs = []
        for j in range(SSD_HPG // 2):
            xpair = xdt_b[:, g * SSD_GW + j * LANES:g * SSD_GW + (j + 1) * LANES]
            ypair = None
            for half in range(2):
                ln = lane0 + g * SSD_HPG + 2 * j + half
                seg = a_cs[:, ln:ln + 1] - a_cs_t[ln:ln + 1, :]
                decay = jnp.exp(jnp.where(keep, seg, -jnp.inf))
                mixer = (cb * decay).astype(BF16)
                xh = jnp.where(lo_half if half == 0 else jnp.logical_not(lo_half), xpair,
                               jnp.zeros_like(xpair))
                part = _dot(mixer, xh)
                ypair = part if ypair is None else ypair + part
            pairs.append(ypair)
        y_groups.append(jnp.concatenate(pairs, axis=1) + y_off)
    return jnp.concatenate(y_groups, axis=1), xs


def _ssd_fwd_kernel(cur_ref, left_ref, right_ref, cw_ref, cb_ref, dtb_ref, alog_ref, exp_ref,
                    y_ref, xc_ref, state_ref, buf_ref):
    q = SSD_CHUNK
    c = pl.program_id(1)
    nc = pl.num_programs(1)

    @pl.when(c == 0)
    def _():
        state_ref[...] = jnp.zeros_like(state_ref)

    xbc_lo, xbc_hi = SSD_WIDTH, SSD_WIDTH + SSD_XBC
    cw = cw_ref[...]
    for bi in range(SSD_BATCH):
        blk = cur_ref[bi]
        buf_ref[bi, 0:SSD_HALO, :] = jnp.where(c > 0, left_ref[bi][:, xbc_lo:xbc_hi], 0.0)
        buf_ref[bi, SSD_HALO:SSD_HALO + q, :] = blk[:, xbc_lo:xbc_hi]
        buf_ref[bi, SSD_HALO + q:, :] = jnp.where(c < nc - 1, right_ref[bi][:, xbc_lo:xbc_hi], 0.0)
        acc = jnp.broadcast_to(cb_ref[...], (q, SSD_XBC))
        for k in range(SSD_CONV_WIDTH):
            start = SSD_HALO - SSD_CONV_PAD + k
            acc = acc + cw[k:k + 1, :] * buf_ref[bi, start:start + q, :]
        xbc = _silu(acc)
        xc_ref[bi] = xbc
        y, _ = _ssd_chunk(xbc, blk[:, xbc_hi:], dtb_ref[...], alog_ref[...], exp_ref[...],
                          state_ref, bi, reverse=False)
        y_ref[bi] = y


def _ssd_bwd_kernel(xc_ref, dt_ref, z_ref, yf_ref, dtb_ref, alog_ref, exp_ref, dskip_ref, ng_ref,
                    o_ref, state_ref):
    c = pl.program_id(1)

    @pl.when(c == 0)
    def _():
        state_ref[...] = jnp.zeros_like(state_ref)

    for bi in range(SSD_BATCH):
        y, xs = _ssd_chunk(xc_ref[bi], dt_ref[bi], dtb_ref[...], alog_ref[...], exp_ref[...],
                           state_ref, bi, reverse=True)
        y = y + yf_ref[bi] + dskip_ref[...] * xs
        o_ref[bi] = _rms(y * _silu(z_ref[bi]), ng_ref[...]).astype(o_ref.dtype)


def ssd_mixer(ps, cw, cb, dtb, alog, expand_f, expand_b, dskip, ng):
    bsz, s, _ = ps.shape
    q, nb = SSD_CHUNK, SSD_BATCH
    assert bsz % nb == 0 and s % q == 0
    nc = s // q
    hb = q // SSD_HALO
    last_halo = s // SSD_HALO - 1
    full = lambda shape: pl.BlockSpec(shape, lambda i, c: (0,) * len(shape))
    params = [full((1, LANES)), full((1, LANES)), full((LANES, SSD_WIDTH))]
    state = pltpu.VMEM((nb, SSD_STATE, SSD_WIDTH), F32)
    fwd = lambda width, col=0: pl.BlockSpec((nb, q, width), lambda i, c: (i, c, col))
    y_f, xc = pl.pallas_call(
        _ssd_fwd_kernel,
        grid=(bsz // nb, nc),
        in_specs=[fwd(SSD_SLAB),
                  pl.BlockSpec((nb, SSD_HALO, SSD_SLAB),
                               lambda i, c: (i, jnp.maximum(c * hb - 1, 0), 0)),
                  pl.BlockSpec((nb, SSD_HALO, SSD_SLAB),
                               lambda i, c: (i, jnp.minimum((c + 1) * hb, last_halo), 0)),
                  full((SSD_CONV_WIDTH, SSD_XBC)), full((1, SSD_XBC))] + params,
        out_specs=[fwd(SSD_WIDTH), fwd(SSD_XBC)],
        out_shape=[jax.ShapeDtypeStruct((bsz, s, SSD_WIDTH), F32),
                   jax.ShapeDtypeStruct((bsz, s, SSD_XBC), F32)],
        scratch_shapes=[state, pltpu.VMEM((nb, q + 2 * SSD_HALO, SSD_XBC), F32)],
        compiler_params=_cparams(("parallel", "arbitrary")),
        name="ssd_fwd",
    )(ps, ps, ps, cw, cb, dtb, alog, expand_f)
    bwd = lambda width, col=0: pl.BlockSpec((nb, q, width), lambda i, c: (i, nc - 1 - c, col))
    dt_col = (SSD_WIDTH + SSD_XBC) // SSD_DT_COLS
    return pl.pallas_call(
        _ssd_bwd_kernel,
        grid=(bsz // nb, nc),
        in_specs=[bwd(SSD_XBC), bwd(SSD_DT_COLS, dt_col), bwd(SSD_WIDTH), bwd(SSD_WIDTH)]
                 + params + [full((1, SSD_WIDTH)), full((1, SSD_WIDTH))],
        out_specs=bwd(SSD_WIDTH),
        out_shape=jax.ShapeDtypeStruct((bsz, s, SSD_WIDTH), BF16),
        scratch_shapes=[state],
        compiler_params=_cparams(("parallel", "arbitrary")),
        name="ssd_bwd",
    )(xc, ps, ps, y_f, dtb, alog, expand_b, dskip, ng)


def _bias_tiles_kernel(bucket_ref, table_ref, o_ref):
    h = pl.program_id(0)
    bucket = bucket_ref[0]
    out = jnp.zeros(bucket.shape, F32)
    for b in range(REL_BUCKETS):
        out = jnp.where(bucket == b, table_ref[b, h] * LOG2E, out)
    o_ref[0, 0] = out


def bias_tiles(buckets, rel_bias):
    nt, t, _ = buckets.shape
    return pl.pallas_call(
        _bias_tiles_kernel,
        grid=(ATT_HEADS, nt),
        in_specs=[pl.BlockSpec((1, t, t), lambda h, d: (d, 0, 0)),
                  pl.BlockSpec(memory_space=pltpu.SMEM)],
        out_specs=pl.BlockSpec((1, 1, t, t), lambda h, d: (h, d, 0, 0)),
        out_shape=jax.ShapeDtypeStruct((ATT_HEADS, nt, t, t), F32),
        compiler_params=_cparams(("parallel", "parallel")),
        name="bias_tiles",
    )(buckets, rel_bias)


def _rel_bucket(rel):
    nb = REL_BUCKETS // 2
    max_exact = nb // 2
    ret = jnp.where(rel > 0, nb, 0)
    n = jnp.abs(rel)
    nf = jnp.maximum(n, 1).astype(jnp.float32)
    large = max_exact + (jnp.log(nf / max_exact) / math.log(REL_MAX_DIST / max_exact)
                         * (nb - max_exact)).astype(jnp.int32)
    large = jnp.minimum(large, nb - 1)
    return ret + jnp.where(n < max_exact, n, large)


def _bucket_tiles(t):
    i = jnp.arange(t, dtype=jnp.int32)
    half = N_BIAS_TILES // 2
    offs = (jnp.arange(N_BIAS_TILES, dtype=jnp.int32) - half) * t
    rel = offs[:, None, None] + i[None, None, :] - i[None, :, None]
    return _rel_bucket(rel).astype(jnp.int32)


def _attn_kernel(q_ref, k_ref, v_ref, bias_ref, lam_ref, g_ref, o_ref,
                 qz_ref, s_ref, p_ref, m_ref, l_ref, alpha_ref, acc_ref, *, lam_init, nk):
    t = ATT_TILE
    tq = ATT_QTILES * t
    qi = pl.program_id(2)
    qb = q_ref[0]
    lane = lax.broadcasted_iota(jnp.int32, (tq, 2 * ATT_HEAD_DIM), 1)
    qz_ref[0] = jnp.where(lane < ATT_HEAD_DIM, qb, jnp.zeros_like(qb))
    qz_ref[1] = jnp.where(lane >= ATT_HEAD_DIM, qb, jnp.zeros_like(qb))
    m_ref[...] = jnp.full(m_ref.shape, -jnp.inf, F32)
    l_ref[...] = jnp.zeros_like(l_ref)
    acc_ref[...] = jnp.zeros_like(acc_ref)
    half = N_BIAS_TILES // 2
    rb = ATT_ROWS

    def key_rows(kj):
        start = kj * t
        return pl.ds(start if isinstance(start, int) else pl.multiple_of(start, t), t)

    def apply_pv(slot, kj):
        vb = v_ref[0, key_rows(kj), :]
        for mp in range(2):
            acc_ref[mp] = alpha_ref[slot, mp] * acc_ref[mp] + _dot(p_ref[slot, mp], vb)

    def pair(kk, carry, first=False):
        for slot in range(2):
            kj = 2 * kk + slot
            kb = k_ref[0, key_rows(kj), :]
            tiles = [jnp.clip(kj - (ATT_QTILES * qi + u), -half, half) + half
                     for u in range(ATT_QTILES)]
            for mp in range(2):
                s_ref[slot, mp] = _dot_nt(qz_ref[mp], kb)
            if not (first and slot == 0):
                apply_pv(1 - slot, kj - 1)
            for mp in range(2):
                for r0 in range(0, tq, rb):
                    rows = slice(r0, r0 + rb)
                    bias_rows = slice(r0 % t, r0 % t + rb)
                    s = s_ref[slot, mp, rows, :] + bias_ref[0, tiles[r0 // t], bias_rows, :]
                    m_old = m_ref[mp, rows, :]
                    m_new = jnp.maximum(m_old, jnp.max(s, axis=-1, keepdims=True))
                    alpha = jnp.exp2(m_old - m_new)
                    p = jnp.exp2(s - jnp.concatenate([m_new] * (t // LANES), axis=1))
                    l_ref[mp, rows, :] = (alpha * l_ref[mp, rows, :]
                                          + jnp.sum(p, axis=-1, keepdims=True))
                    m_ref[mp, rows, :] = m_new
                    alpha_ref[slot, mp, rows, :] = alpha
                    p_ref[slot, mp, rows, :] = p.astype(BF16)
        return carry

    pair(0, 0, first=True)
    lax.fori_loop(1, nk // 2, pair, 0)
    apply_pv(1, nk - 1)

    lp = lam_ref[...]
    lam = (jnp.exp(jnp.sum(lp[0:1] * lp[1:2], axis=-1, keepdims=True))
           - jnp.exp(jnp.sum(lp[2:3] * lp[3:4], axis=-1, keepdims=True)) + lam_init)
    o = acc_ref[0] / l_ref[0] - lam * (acc_ref[1] / l_ref[1])
    o_ref[0] = (_rms(o, g_ref[...]) * (1.0 - lam_init)).astype(o_ref.dtype)


def diff_attention(qkv, bias, lam_params, subln_g, lam_init):
    bsz, s, _ = qkv.shape
    t = ATT_TILE
    tq = ATT_QTILES * t
    nk = s // t
    hw = 2 * ATT_HEAD_DIM
    assert nk % 2 == 0 and s % tq == 0
    stat = pltpu.VMEM((2, tq, LANES), F32)
    return pl.pallas_call(
        functools.partial(_attn_kernel, lam_init=lam_init, nk=nk),
        grid=(bsz, ATT_HEADS, s // tq),
        in_specs=[
            pl.BlockSpec((1, tq, hw), lambda b, h, i: (b, i, h)),
            pl.BlockSpec((1, s, hw), lambda b, h, i: (b, 0, ATT_HEADS + h)),
            pl.BlockSpec((1, s, hw), lambda b, h, i: (b, 0, 2 * ATT_HEADS + h)),
            pl.BlockSpec((1, N_BIAS_TILES, t, t), lambda b, h, i: (h, 0, 0, 0)),
            pl.BlockSpec((4, ATT_HEAD_DIM), lambda b, h, i: (0, 0)),
            pl.BlockSpec((1, hw), lambda b, h, i: (0, 0)),
        ],
        out_specs=pl.BlockSpec((1, tq, hw), lambda b, h, i: (b, i, h)),
        out_shape=jax.ShapeDtypeStruct((bsz, s, ATT_WIDTH), BF16),
        scratch_shapes=[pltpu.VMEM((2, tq, hw), BF16),
                        pltpu.VMEM((2, 2, tq, t), F32),
                        pltpu.VMEM((2, 2, tq, t), BF16),
                        stat, stat,
                        pltpu.VMEM((2, 2, tq, LANES), F32),
                        pltpu.VMEM((2, tq, hw), F32)],
        compiler_params=_cparams(("parallel", "parallel", "parallel")),
        name="diff_attention",
    )(qkv, qkv, qkv, bias, lam_params, subln_g)


def _out_proj_kernel(yc_ref, ys_ref, ya_ref, h_ref, w_ref, o_ref):
    c0, c1 = CONV_CH, CONV_CH + SSD_WIDTH
    acc = _dot(yc_ref[...], w_ref[0:c0, :])
    acc = acc + _dot(ys_ref[...], w_ref[c0:c1, :])
    acc = acc + _dot(ya_ref[...], w_ref[c1:, :])
    o_ref[...] = h_ref[...] + acc


def out_proj(yc, ys, ya, h, w, tm=ROW_TILE):
    m, d = h.shape
    row = lambda n: pl.BlockSpec((tm, n), lambda i: (i, 0))
    return pl.pallas_call(
        _out_proj_kernel,
        grid=(m // tm,),
        in_specs=[row(CONV_CH), row(SSD_WIDTH), row(ATT_WIDTH), row(d),
                  pl.BlockSpec(w.shape, lambda i: (0, 0))],
        out_specs=row(d),
        out_shape=jax.ShapeDtypeStruct((m, d), F32),
        compiler_params=_cparams(("parallel",)),
        name="out_proj",
    )(yc, ys, ya, h, w)


def _mlp_kernel(h_ref, g_ref, wu_ref, wd_ref, o_ref, xn_ref, acc_ref):
    f = pl.program_id(1)

    @pl.when(f == 0)
    def _():
        xn_ref[...] = _rms(h_ref[...], g_ref[...]).astype(BF16)
        acc_ref[...] = jnp.zeros_like(acc_ref)

    hid = jnp.square(jnp.maximum(_dot(xn_ref[...], wu_ref[...]), 0.0)).astype(BF16)
    acc_ref[...] += _dot(hid, wd_ref[...])

    @pl.when(f == pl.num_programs(1) - 1)
    def _():
        o_ref[...] = h_ref[...] + acc_ref[...]


def mlp(h, g, wu, wd, tm=ROW_TILE, tf=FF_TILE):
    m, d = h.shape
    ff = wu.shape[1]
    return pl.pallas_call(
        _mlp_kernel,
        grid=(m // tm, ff // tf),
        in_specs=[pl.BlockSpec((tm, d), lambda i, f: (i, 0)),
                  pl.BlockSpec((1, d), lambda i, f: (0, 0)),
                  pl.BlockSpec((d, tf), lambda i, f: (0, f)),
                  pl.BlockSpec((tf, d), lambda i, f: (f, 0))],
        out_specs=pl.BlockSpec((tm, d), lambda i, f: (i, 0)),
        out_shape=jax.ShapeDtypeStruct((m, d), F32),
        scratch_shapes=[pltpu.VMEM((tm, d), BF16), pltpu.VMEM((tm, d), F32)],
        compiler_params=_cparams(("parallel", "arbitrary")),
        name="mlp",
    )(h, g, wu, wd)


def _ple_kernel(h_ref, p_ref, g_ref, wg_ref, wp_ref, gf_ref, o_ref, *, final):
    h = h_ref[...]
    gate = jax.nn.sigmoid(_dot(_rms(h, g_ref[...]).astype(BF16), wg_ref[...]))
    out = h + _dot(p_ref[...].astype(BF16), wp_ref[...]) * gate
    if final:
        out = _rms(out, gf_ref[...])
    o_ref[...] = out


def ple(h, p, g, wg, wp, gf, final, tm=ROW_TILE):
    m, d = h.shape
    vec = pl.BlockSpec((1, d), lambda i: (0, 0))
    return pl.pallas_call(
        functools.partial(_ple_kernel, final=final),
        grid=(m // tm,),
        in_specs=[pl.BlockSpec((tm, d), lambda i: (i, 0)),
                  pl.BlockSpec((tm, PLE_DIM), lambda i: (i, 0)),
                  vec,
                  pl.BlockSpec((d, d), lambda i: (0, 0)),
                  pl.BlockSpec((PLE_DIM, d), lambda i: (0, 0)),
                  vec],
        out_specs=pl.BlockSpec((tm, d), lambda i: (i, 0)),
        out_shape=jax.ShapeDtypeStruct((m, d), F32),
        compiler_params=_cparams(("parallel",)),
        name="ple_final" if final else "ple",
    )(h, p, g, wg, wp, gf)


def _expand_matrix(direction):
    e = np.zeros((LANES, SSD_WIDTH), np.float32)
    for hd in range(SSD_HEADS):
        e[direction * SSD_HEADS + hd, hd * SSD_HEAD_DIM:(hd + 1) * SSD_HEAD_DIM] = 1.0
    return jnp.asarray(e)


def _pad_lanes(v):
    return jnp.pad(v.reshape(1, -1), ((0, 0), (0, LANES - v.size)))


def _layer_params(i, norm_mix_g, w_in, conv_w, conv_b, conv_norm_g, conv_norm_b, ssd_conv_w,
                  ssd_conv_b, ssd_dt_bias, ssd_a_log, ssd_d, ssd_norm_g, lambda_q1, lambda_k1,
                  lambda_q2, lambda_k2, attn_subln_g, w_out, norm_mlp_g, w_up, w_down,
                  norm_ple_g, w_ple, w_ple_gate):
    row = lambda v: v.reshape(1, -1)
    w = w_in[i]
    c_conv = 2 * CONV_CH
    c_dt = c_conv + SSD_WIDTH + SSD_XBC
    c_att = c_dt + 2 * SSD_HEADS
    w_ssd = jnp.pad(w[:, c_conv:c_att], ((0, 0), (0, SSD_DT_COLS - 2 * SSD_HEADS)))
    return dict(
        norm_mix_g=row(norm_mix_g[i]),
        w_conv=w[:, :c_conv].astype(BF16),
        w_ssd=w_ssd.astype(BF16),
        w_att=jnp.concatenate([w[:, c_att:c_att + ATT_WIDTH] * (LOG2E * ATT_HEAD_DIM ** -0.5),
                               w[:, c_att + ATT_WIDTH:]], axis=1).astype(BF16),
        conv_w=conv_w[i], conv_b=row(conv_b[i]), conv_g=row(conv_norm_g[i]),
        conv_beta=row(conv_norm_b[i]),
        ssd_cw=ssd_conv_w[i], ssd_cb=row(ssd_conv_b[i]),
        ssd_dtb=_pad_lanes(ssd_dt_bias[i]), ssd_alog=_pad_lanes(ssd_a_log[i]),
        ssd_dskip=row(jnp.repeat(ssd_d[i], SSD_HEAD_DIM)), ssd_ng=row(ssd_norm_g[i]),
        lam=jnp.stack([lambda_q1[i], lambda_k1[i], lambda_q2[i], lambda_k2[i]]),
        subln_g=row(attn_subln_g[i]),
        w_out=w_out[i].astype(BF16),
        norm_mlp_g=row(norm_mlp_g[i]),
        w_up=w_up[i].astype(BF16), w_down=w_down[i].astype(BF16),
        norm_ple_g=row(norm_ple_g[i]),
        w_ple=w_ple[i].astype(BF16), w_gate=w_ple_gate[i].astype(BF16),
    )


def _run(x, p, layers, bias, expand_f, expand_b, final_g):
    bsz, s, d = x.shape
    m = bsz * s
    h = x.reshape(m, d)
    for i, lp in enumerate(layers):
        lam_init = 0.8 - 0.6 * math.exp(-0.3 * i)
        pc, ps, pa = norm_proj(h, lp["norm_mix_g"], lp["w_conv"], lp["w_ssd"], lp["w_att"])
        pc = pc.reshape(bsz, s, CONV_SLAB)
        ps = ps.reshape(bsz, s, SSD_SLAB)
        pa = pa.reshape(bsz, s, ATT_SLAB)
        y_conv = conv_module(pc, lp["conv_w"], lp["conv_b"], lp["conv_g"], lp["conv_beta"])
        y_ssd = ssd_mixer(ps, lp["ssd_cw"], lp["ssd_cb"], lp["ssd_dtb"], lp["ssd_alog"],
                          expand_f, expand_b, lp["ssd_dskip"], lp["ssd_ng"])
        y_att = diff_attention(pa, bias, lp["lam"], lp["subln_g"], lam_init)
        h = out_proj(y_conv.reshape(m, CONV_CH), y_ssd.reshape(m, SSD_WIDTH),
                     y_att.reshape(m, ATT_WIDTH), h, lp["w_out"])
        h = mlp(h, lp["norm_mlp_g"], lp["w_up"], lp["w_down"])
        h = ple(h, p[i].reshape(m, PLE_DIM), lp["norm_ple_g"], lp["w_gate"], lp["w_ple"],
                final_g, final=(i == len(layers) - 1))
    return h.reshape(bsz, s, d)


def kernel(x_prompt, x_sample, p_prompt, p_sample, norm_mix_g, w_in, conv_w, conv_b, conv_norm_g,
           conv_norm_b, ssd_conv_w, ssd_conv_b, ssd_dt_bias, ssd_a_log, ssd_d, ssd_norm_g,
           lambda_q1, lambda_k1, lambda_q2, lambda_k2, attn_subln_g, rel_bias, w_out, norm_mlp_g,
           w_up, w_down, norm_ple_g, w_ple, w_ple_gate, final_norm_g):
    layers = [_layer_params(i, norm_mix_g, w_in, conv_w, conv_b, conv_norm_g, conv_norm_b,
                            ssd_conv_w, ssd_conv_b, ssd_dt_bias, ssd_a_log, ssd_d, ssd_norm_g,
                            lambda_q1, lambda_k1, lambda_q2, lambda_k2, attn_subln_g, w_out,
                            norm_mlp_g, w_up, w_down, norm_ple_g, w_ple, w_ple_gate)
              for i in range(DEPTH)]
    bias = bias_tiles(_bucket_tiles(ATT_TILE), rel_bias)
    expand_f, expand_b = _expand_matrix(0), _expand_matrix(1)
    final_g = final_norm_g.reshape(1, -1)
    y_prompt = _run(x_prompt, p_prompt, layers, bias, expand_f, expand_b, final_g)
    y_sample = _run(x_sample, p_sample, layers, bias, expand_f, expand_b, final_g)
    return (y_prompt, y_sample)
```

```python
import functools
import math

import jax
import jax.numpy as jnp
import numpy as np
from jax import lax
from jax.experimental import pallas as pl
from jax.experimental.pallas import tpu as pltpu

F32 = jnp.float32
BF16 = jnp.bfloat16

D_MODEL = 2048
DEPTH = 2
PLE_DIM = 256
EPS = 1e-6
CONV_CH = 512
CONV_WIDTH = 31
CONV_PAD = (CONV_WIDTH - 1) // 2
SSD_HEAD_DIM = 64
SSD_HEADS = 12
SSD_WIDTH = SSD_HEADS * SSD_HEAD_DIM
SSD_GROUPS = 2
SSD_HPG = SSD_HEADS // SSD_GROUPS
SSD_STATE = 128
SSD_CONV_WIDTH = 5
SSD_CONV_PAD = (SSD_CONV_WIDTH - 1) // 2
SSD_CHUNK = 128
SSD_XBC = SSD_WIDTH + 2 * SSD_GROUPS * SSD_STATE
ATT_HEADS = 6
ATT_HEAD_DIM = 64
ATT_WIDTH = ATT_HEADS * 2 * ATT_HEAD_DIM
REL_BUCKETS = 32
REL_MAX_DIST = 128
D_FF = 4 * D_MODEL
LOG2E = math.log2(math.e)

LANES = 128
SUBLANES = 8
VMEM_LIMIT = 56 * 1024 * 1024

SSD_DT_COLS = LANES
SSD_SLAB = SSD_WIDTH + SSD_XBC + SSD_DT_COLS
CONV_SLAB = 2 * CONV_CH
ATT_SLAB = 3 * ATT_WIDTH
SSD_GW = SSD_HPG * SSD_HEAD_DIM

ATT_TILE = 512
ATT_QTILES = 2
N_BIAS_TILES = 5
ATT_ROWS = 32
CONV_TILE = 128
CONV_HALO = 16
CONV_ROWS = 32
SSD_HALO = SUBLANES
SSD_BATCH = 4
ROW_TILE = 512
FF_TILE = 1024


def _cparams(sem):
    return pltpu.CompilerParams(dimension_semantics=sem, vmem_limit_bytes=VMEM_LIMIT)


def _rms(x, g):
    ms = jnp.mean(x * x, axis=-1, keepdims=True)
    return (x * lax.rsqrt(ms + EPS)) * g


def _silu(x):
    return x * jax.nn.sigmoid(x)


def _dot(a, b):
    return jnp.dot(a, b, preferred_element_type=F32)


def _split3(x):
    hi = x.astype(BF16)
    r1 = x - hi.astype(F32)
    mid = r1.astype(BF16)
    lo = (r1 - mid.astype(F32)).astype(BF16)
    return hi, mid, lo


def _select_dot(sel, x):
    sel = sel.astype(BF16)
    hi, mid, lo = _split3(x)
    return _dot(sel, hi) + (_dot(sel, mid) + _dot(sel, lo))


def _dot_select(x, sel):
    sel = sel.astype(BF16)
    hi = x.astype(BF16)
    mid = (x - hi.astype(F32)).astype(BF16)
    return _dot(hi, sel) + _dot(mid, sel)


def _dot_nt(a, b):
    return lax.dot_general(a, b, (((1,), (1,)), ((), ())), preferred_element_type=F32)


def _conv_tile(buf_ref, shift_ref, w, b_ref, g_ref, beta_ref, o_ref, row0):
    span = CONV_TILE + 2 * CONV_HALO - SUBLANES
    for b in range(1, SUBLANES):
        shift_ref[b - 1] = buf_ref[b:b + span, :]
    for r0 in range(0, CONV_TILE, CONV_ROWS):
        acc = jnp.broadcast_to(b_ref[...], (CONV_ROWS, CONV_CH))
        for k in range(CONV_WIDTH):
            start = r0 + CONV_HALO - CONV_PAD + k
            b, base = start % SUBLANES, start - start % SUBLANES
            window = (buf_ref[base:base + CONV_ROWS, :] if b == 0
                      else shift_ref[b - 1, base:base + CONV_ROWS, :])
            acc = acc + w[k:k + 1, :] * window
        mu = jnp.mean(acc, axis=-1, keepdims=True)
        xc = acc - mu
        var = jnp.mean(xc * xc, axis=-1, keepdims=True)
        y = xc * lax.rsqrt(var + EPS) * g_ref[...] + beta_ref[...]
        o_ref[row0 + r0:row0 + r0 + CONV_ROWS, :] = _silu(y).astype(o_ref.dtype)


def _proj_conv_kernel(x_ref, g_ref, wc_ref, ws_ref, wa_ref, cw_ref, cb_ref, cg_ref, cbeta_ref,
                      os_ref, oa_ref, oy_ref, ubuf_ref, unext_ref, buf_ref, shift_ref,
                      *, tiles_per_seq):
    tm, halo = ROW_TILE, CONV_HALO
    i = pl.program_id(0)

    @pl.when(i == 0)
    def _():
        ubuf_ref[...] = jnp.zeros_like(ubuf_ref)
        unext_ref[...] = jnp.zeros_like(unext_ref)

    ubuf_ref[0:halo, :] = ubuf_ref[tm:tm + halo, :]
    ubuf_ref[halo:halo + tm, :] = unext_ref[...]

    xn = _rms(x_ref[...], g_ref[...]).astype(BF16)
    pc = _dot(xn, wc_ref[...])
    u = pc[:, :CONV_CH] * jax.nn.sigmoid(pc[:, CONV_CH:])
    unext_ref[...] = u
    ubuf_ref[halo + tm:, :] = u[0:halo, :]
    os_ref[...] = _dot(xn, ws_ref[...])
    oa_ref[...] = _dot(xn, wa_ref[...]).astype(oa_ref.dtype)

    pos = lax.rem(i + tiles_per_seq - 1, tiles_per_seq)
    w = cw_ref[...]
    n_sub = tm // CONV_TILE
    for st in range(n_sub):
        base = st * CONV_TILE
        buf_ref[...] = ubuf_ref[base:base + CONV_TILE + 2 * halo, :]
        if st == 0:
            buf_ref[0:halo, :] = jnp.where(pos > 0, ubuf_ref[0:halo, :], 0.0)
        if st == n_sub - 1:
            buf_ref[halo + CONV_TILE:, :] = jnp.where(pos < tiles_per_seq - 1,
                                                      ubuf_ref[halo + tm:, :], 0.0)
        _conv_tile(buf_ref, shift_ref, w, cb_ref, cg_ref, cbeta_ref, oy_ref, base)


def proj_conv(x, g, w_conv, w_ssd, w_att, cw, cb, cg, cbeta, seq_len, tm=ROW_TILE):
    m, d = x.shape
    assert seq_len % tm == 0 and m % seq_len == 0
    n = m // tm
    resident = lambda w: pl.BlockSpec(w.shape, lambda i: (0, 0), pipeline_mode=pl.Buffered(1))
    cur = lambda cols: pl.BlockSpec((tm, cols), lambda i: (jnp.minimum(i, n - 1), 0))
    vec = lambda: pl.BlockSpec((1, CONV_CH), lambda i: (0, 0))
    return pl.pallas_call(
        functools.partial(_proj_conv_kernel, tiles_per_seq=seq_len // tm),
        grid=(n + 1,),
        in_specs=[cur(d), pl.BlockSpec((1, d), lambda i: (0, 0)),
                  resident(w_conv), resident(w_ssd), resident(w_att),
                  pl.BlockSpec((CONV_WIDTH, CONV_CH), lambda i: (0, 0)), vec(), vec(), vec()],
        out_specs=[cur(SSD_SLAB), cur(ATT_SLAB),
                   pl.BlockSpec((tm, CONV_CH), lambda i: (jnp.maximum(i - 1, 0), 0))],
        out_shape=[jax.ShapeDtypeStruct((m, SSD_SLAB), F32),
                   jax.ShapeDtypeStruct((m, ATT_SLAB), BF16),
                   jax.ShapeDtypeStruct((m, CONV_CH), BF16)],
        scratch_shapes=[pltpu.VMEM((tm + 2 * CONV_HALO, CONV_CH), F32),
                        pltpu.VMEM((tm, CONV_CH), F32),
                        pltpu.VMEM((CONV_TILE + 2 * CONV_HALO, CONV_CH), F32),
                        pltpu.VMEM((SUBLANES - 1, CONV_TILE + 2 * CONV_HALO - SUBLANES, CONV_CH),
                                   F32)],
        compiler_params=_cparams(("arbitrary",)),
        name="proj_conv",
    )(x, g, w_conv, w_ssd, w_att, cw, cb, cg, cbeta)


def _ssd_chunk(xbc, dt_raw, dtb, alog, expand, state_ref, bi, *, reverse):
    q = SSD_CHUNK
    xs = xbc[:, :SSD_WIDTH]
    bmat = xbc[:, SSD_WIDTH:SSD_WIDTH + SSD_GROUPS * SSD_STATE]
    cmat = xbc[:, SSD_WIDTH + SSD_GROUPS * SSD_STATE:]

    dt = jax.nn.softplus(dt_raw + dtb)
    dta = dt * (-jnp.exp(alog))
    row = lax.broadcasted_iota(jnp.int32, (q, q), 0)
    col = lax.broadcasted_iota(jnp.int32, (q, q), 1)
    keep = (row <= col) if reverse else (row >= col)
    a_cs = _select_dot(keep, dta)
    a_tot = jnp.sum(dta, axis=0, keepdims=True)
    ea = jnp.exp(a_cs)
    dec = jnp.exp(a_tot - a_cs)
    cdec = jnp.broadcast_to(jnp.exp(a_tot), (SUBLANES, LANES))
    ex = _dot_select(jnp.concatenate([dt, ea, dec, cdec], axis=0), expand)
    dtx, eax, decx, cdecx = ex[0:q], ex[q:2 * q], ex[2 * q:3 * q], ex[3 * q:3 * q + 1]
    xdt = xs * dtx
    xdt_b = xdt.astype(BF16)
    xd_b = (xdt * decx).astype(BF16)
    a_cs_t = a_cs.T
    lane = lax.broadcasted_iota(jnp.int32, (q, LANES), 1)
    lo_half = lane < SSD_HEAD_DIM
    lane0 = SSD_HEADS if reverse else 0

    y_groups = []
    for g in range(SSD_GROUPS):
        gs = slice(g * SSD_GW, (g + 1) * SSD_GW)
        bg = bmat[:, g * SSD_STATE:(g + 1) * SSD_STATE]
        cg_b = cmat[:, g * SSD_STATE:(g + 1) * SSD_STATE].astype(BF16)
        cb = _dot_nt(cg_b, bg.astype(BF16))
        prev = state_ref[bi, :, gs]
        y_off = _dot(cg_b, prev.astype(BF16)) * eax[:, gs]
        new_states = _dot(bg.T.astype(BF16), xd_b[:, gs])
        state_ref[bi, :, gs] = prev * cdecx[:, gs] + new_states
        pairs = []
        for j in range(SSD_HPG // 2):
            xpair = xdt_b[:, g * SSD_GW + j * LANES:g * SSD_GW + (j + 1) * LANES]
            ypair = None
            for half in range(2):
                ln = lane0 + g * SSD_HPG + 2 * j + half
                seg = a_cs[:, ln:ln + 1] - a_cs_t[ln:ln + 1, :]
                decay = jnp.exp(jnp.where(keep, seg, -jnp.inf))
                mixer = (cb * decay).astype(BF16)
                xh = jnp.where(lo_half if half == 0 else jnp.logical_not(lo_half), xpair,
                               jnp.zeros_like(xpair))
                part = _dot(mixer, xh)
                ypair = part if ypair is None else ypair + part
            pairs.append(ypair)
        y_groups.append(jnp.concatenate(pairs, axis=1) + y_off)
    return jnp.concatenate(y_groups, axis=1), xs


def _ssd_fwd_kernel(cur_ref, left_ref, right_ref, cw_ref, cb_ref, dtb_ref, alog_ref, exp_ref,
                    y_ref, xc_ref, state_ref, buf_ref):
    q = SSD_CHUNK
    c = pl.program_id(1)
    nc = pl.num_programs(1)

    @pl.when(c == 0)
    def _():
        state_ref[...] = jnp.zeros_like(state_ref)

    xbc_lo, xbc_hi = SSD_WIDTH, SSD_WIDTH + SSD_XBC
    cw = cw_ref[...]
    for bi in range(SSD_BATCH):
        blk = cur_ref[bi]
        buf_ref[bi, 0:SSD_HALO, :] = jnp.where(c > 0, left_ref[bi][:, xbc_lo:xbc_hi], 0.0)
        buf_ref[bi, SSD_HALO:SSD_HALO + q, :] = blk[:, xbc_lo:xbc_hi]
        buf_ref[bi, SSD_HALO + q:, :] = jnp.where(c < nc - 1, right_ref[bi][:, xbc_lo:xbc_hi], 0.0)
        acc = jnp.broadcast_to(cb_ref[...], (q, SSD_XBC))
        for k in range(SSD_CONV_WIDTH):
            start = SSD_HALO - SSD_CONV_PAD + k
            acc = acc + cw[k:k + 1, :] * buf_ref[bi, start:start + q, :]
        xbc = _silu(acc)
        xc_ref[bi] = xbc
        y, _ = _ssd_chunk(xbc, blk[:, xbc_hi:], dtb_ref[...], alog_ref[...], exp_ref[...],
                          state_ref, bi, reverse=False)
        y_ref[bi] = y


def _ssd_bwd_kernel(xc_ref, dt_ref, z_ref, yf_ref, dtb_ref, alog_ref, exp_ref, dskip_ref, ng_ref,
                    o_ref, state_ref):
    c = pl.program_id(1)

    @pl.when(c == 0)
    def _():
        state_ref[...] = jnp.zeros_like(state_ref)

    for bi in range(SSD_BATCH):
        y, xs = _ssd_chunk(xc_ref[bi], dt_ref[bi], dtb_ref[...], alog_ref[...], exp_ref[...],
                           state_ref, bi, reverse=True)
        y = y + yf_ref[bi] + dskip_ref[...] * xs
        o_ref[bi] = _rms(y * _silu(z_ref[bi]), ng_ref[...]).astype(o_ref.dtype)


def ssd_mixer(ps, cw, cb, dtb, alog, expand_f, expand_b, dskip, ng):
    bsz, s, _ = ps.shape
    q, nb = SSD_CHUNK, SSD_BATCH
    assert bsz % nb == 0 and s % q == 0
    nc = s // q
    hb = q // SSD_HALO
    last_halo = s // SSD_HALO - 1
    full = lambda shape: pl.BlockSpec(shape, lambda i, c: (0,) * len(shape))
    params = [full((1, LANES)), full((1, LANES)), full((LANES, SSD_WIDTH))]
    state = pltpu.VMEM((nb, SSD_STATE, SSD_WIDTH), F32)
    fwd = lambda width, col=0: pl.BlockSpec((nb, q, width), lambda i, c: (i, c, col))
    y_f, xc = pl.pallas_call(
        _ssd_fwd_kernel,
        grid=(bsz // nb, nc),
        in_specs=[fwd(SSD_SLAB),
                  pl.BlockSpec((nb, SSD_HALO, SSD_SLAB),
                               lambda i, c: (i, jnp.maximum(c * hb - 1, 0), 0)),
                  pl.BlockSpec((nb, SSD_HALO, SSD_SLAB),
                               lambda i, c: (i, jnp.minimum((c + 1) * hb, last_halo), 0)),
                  full((SSD_CONV_WIDTH, SSD_XBC)), full((1, SSD_XBC))] + params,
        out_specs=[fwd(SSD_WIDTH), fwd(SSD_XBC)],
        out_shape=[jax.ShapeDtypeStruct((bsz, s, SSD_WIDTH), F32),
                   jax.ShapeDtypeStruct((bsz, s, SSD_XBC), F32)],
        scratch_shapes=[state, pltpu.VMEM((nb, q + 2 * SSD_HALO, SSD_XBC), F32)],
        compiler_params=_cparams(("parallel", "arbitrary")),
        name="ssd_fwd",
    )(ps, ps, ps, cw, cb, dtb, alog, expand_f)
    bwd = lambda width, col=0: pl.BlockSpec((nb, q, width), lambda i, c: (i, nc - 1 - c, col))
    dt_col = (SSD_WIDTH + SSD_XBC) // SSD_DT_COLS
    return pl.pallas_call(
        _ssd_bwd_kernel,
        grid=(bsz // nb, nc),
        in_specs=[bwd(SSD_XBC), bwd(SSD_DT_COLS, dt_col), bwd(SSD_WIDTH), bwd(SSD_WIDTH)]
                 + params + [full((1, SSD_WIDTH)), full((1, SSD_WIDTH))],
        out_specs=bwd(SSD_WIDTH),
        out_shape=jax.ShapeDtypeStruct((bsz, s, SSD_WIDTH), BF16),
        scratch_shapes=[state],
        compiler_params=_cparams(("parallel", "arbitrary")),
        name="ssd_bwd",
    )(xc, ps, ps, y_f, dtb, alog, expand_b, dskip, ng)


def _bias_tiles_kernel(bucket_ref, table_ref, o_ref):
    h = pl.program_id(0)
    bucket = bucket_ref[0]
    out = jnp.zeros(bucket.shape, F32)
    for b in range(REL_BUCKETS):
        out = jnp.where(bucket == b, table_ref[b, h] * LOG2E, out)
    o_ref[0, 0] = out


def bias_tiles(buckets, rel_bias):
    nt, t, _ = buckets.shape
    return pl.pallas_call(
        _bias_tiles_kernel,
        grid=(ATT_HEADS, nt),
        in_specs=[pl.BlockSpec((1, t, t), lambda h, d: (d, 0, 0)),
                  pl.BlockSpec(memory_space=pltpu.SMEM)],
        out_specs=pl.BlockSpec((1, 1, t, t), lambda h, d: (h, d, 0, 0)),
        out_shape=jax.ShapeDtypeStruct((ATT_HEADS, nt, t, t), F32),
        compiler_params=_cparams(("parallel", "parallel")),
        name="bias_tiles",
    )(buckets, rel_bias)


def _rel_bucket(rel):
    nb = REL_BUCKETS // 2
    max_exact = nb // 2
    ret = jnp.where(rel > 0, nb, 0)
    n = jnp.abs(rel)
    nf = jnp.maximum(n, 1).astype(jnp.float32)
    large = max_exact + (jnp.log(nf / max_exact) / math.log(REL_MAX_DIST / max_exact)
                         * (nb - max_exact)).astype(jnp.int32)
    large = jnp.minimum(large, nb - 1)
    return ret + jnp.where(n < max_exact, n, large)


def _bucket_tiles(t):
    i = jnp.arange(t, dtype=jnp.int32)
    half = N_BIAS_TILES // 2
    offs = (jnp.arange(N_BIAS_TILES, dtype=jnp.int32) - half) * t
    rel = offs[:, None, None] + i[None, None, :] - i[None, :, None]
    return _rel_bucket(rel).astype(jnp.int32)


def _attn_kernel(q_ref, k_ref, v_ref, bias_ref, lam_ref, g_ref, o_ref,
                 qz_ref, s_ref, p_ref, m_ref, l_ref, alpha_ref, acc_ref, *, lam_init, nk):
    t = ATT_TILE
    tq = ATT_QTILES * t
    qi = pl.program_id(2)
    qb = q_ref[0]
    lane = lax.broadcasted_iota(jnp.int32, (tq, 2 * ATT_HEAD_DIM), 1)
    qz_ref[0] = jnp.where(lane < ATT_HEAD_DIM, qb, jnp.zeros_like(qb))
    qz_ref[1] = jnp.where(lane >= ATT_HEAD_DIM, qb, jnp.zeros_like(qb))
    m_ref[...] = jnp.full(m_ref.shape, -jnp.inf, F32)
    l_ref[...] = jnp.zeros_like(l_ref)
    acc_ref[...] = jnp.zeros_like(acc_ref)
    half = N_BIAS_TILES // 2
    rb = ATT_ROWS

    def key_rows(kj):
        start = kj * t
        return pl.ds(start if isinstance(start, int) else pl.multiple_of(start, t), t)

    def apply_pv(slot, kj):
        vb = v_ref[0, key_rows(kj), :]
        for mp in range(2):
            acc_ref[mp] = alpha_ref[slot, mp] * acc_ref[mp] + _dot(p_ref[slot, mp], vb)

    def logits(slot, kj):
        kb = k_ref[0, key_rows(kj), :]
        for mp in range(2):
            s_ref[slot, mp] = _dot_nt(qz_ref[mp], kb)

    def softmax(slot, kj):
        tiles = [jnp.clip(kj - (ATT_QTILES * qi + u), -half, half) + half
                 for u in range(ATT_QTILES)]
        for mp in range(2):
            for r0 in range(0, tq, rb):
                rows = slice(r0, r0 + rb)
                bias_rows = slice(r0 % t, r0 % t + rb)
                s = s_ref[slot, mp, rows, :] + bias_ref[0, tiles[r0 // t], bias_rows, :]
                m_old = m_ref[mp, rows, :]
                m_new = jnp.maximum(m_old, jnp.max(s, axis=-1, keepdims=True))
                alpha = jnp.exp2(m_old - m_new)
                p = jnp.exp2(s - jnp.concatenate([m_new] * (t // LANES), axis=1))
                l_ref[mp, rows, :] = (alpha * l_ref[mp, rows, :]
                                      + jnp.sum(p, axis=-1, keepdims=True))
                m_ref[mp, rows, :] = m_new
                alpha_ref[slot, mp, rows, :] = alpha
                p_ref[slot, mp, rows, :] = p.astype(BF16)

    def pair(kk, carry, first=False):
        for slot in range(2):
            kj = 2 * kk + slot
            logits(slot, kj)
            if not (first and slot == 0):
                apply_pv(1 - slot, kj - 1)
            softmax(slot, kj)
        return carry

    pair(0, 0, first=True)
    lax.fori_loop(1, nk // 2, pair, 0)
    apply_pv(1, nk - 1)

    lp = lam_ref[...]
    lam = (jnp.exp(jnp.sum(lp[0:1] * lp[1:2], axis=-1, keepdims=True))
           - jnp.exp(jnp.sum(lp[2:3] * lp[3:4], axis=-1, keepdims=True)) + lam_init)
    o = acc_ref[0] / l_ref[0] - lam * (acc_ref[1] / l_ref[1])
    o_ref[0] = (_rms(o, g_ref[...]) * (1.0 - lam_init)).astype(o_ref.dtype)


def diff_attention(qkv, bias, lam_params, subln_g, lam_init):
    bsz, s, _ = qkv.shape
    t = ATT_TILE
    tq = ATT_QTILES * t
    nk = s // t
    hw = 2 * ATT_HEAD_DIM
    assert nk % 2 == 0 and s % tq == 0
    stat = pltpu.VMEM((2, tq, LANES), F32)
    return pl.pallas_call(
        functools.partial(_attn_kernel, lam_init=lam_init, nk=nk),
        grid=(bsz, ATT_HEADS, s // tq),
        in_specs=[
            pl.BlockSpec((1, tq, hw), lambda b, h, i: (b, i, h)),
            pl.BlockSpec((1, s, hw), lambda b, h, i: (b, 0, ATT_HEADS + h)),
            pl.BlockSpec((1, s, hw), lambda b, h, i: (b, 0, 2 * ATT_HEADS + h)),
            pl.BlockSpec((1, N_BIAS_TILES, t, t), lambda b, h, i: (h, 0, 0, 0)),
            pl.BlockSpec((4, ATT_HEAD_DIM), lambda b, h, i: (0, 0)),
            pl.BlockSpec((1, hw), lambda b, h, i: (0, 0)),
        ],
        out_specs=pl.BlockSpec((1, tq, hw), lambda b, h, i: (b, i, h)),
        out_shape=jax.ShapeDtypeStruct((bsz, s, ATT_WIDTH), BF16),
        scratch_shapes=[pltpu.VMEM((2, tq, hw), BF16),
                        pltpu.VMEM((2, 2, tq, t), F32),
                        pltpu.VMEM((2, 2, tq, t), BF16),
                        stat, stat,
                        pltpu.VMEM((2, 2, tq, LANES), F32),
                        pltpu.VMEM((2, tq, hw), F32)],
        compiler_params=_cparams(("parallel", "parallel", "parallel")),
        name="diff_attention",
    )(qkv, qkv, qkv, bias, lam_params, subln_g)


def _out_proj_kernel(yc_ref, ys_ref, ya_ref, h_ref, w_ref, o_ref):
    c0, c1 = CONV_CH, CONV_CH + SSD_WIDTH
    acc = _dot(yc_ref[...], w_ref[0:c0, :])
    acc = acc + _dot(ys_ref[...], w_ref[c0:c1, :])
    acc = acc + _dot(ya_ref[...], w_ref[c1:, :])
    o_ref[...] = h_ref[...] + acc


def out_proj(yc, ys, ya, h, w, tm=ROW_TILE):
    m, d = h.shape
    row = lambda n: pl.BlockSpec((tm, n), lambda i: (i, 0))
    return pl.pallas_call(
        _out_proj_kernel,
        grid=(m // tm,),
        in_specs=[row(CONV_CH), row(SSD_WIDTH), row(ATT_WIDTH), row(d),
                  pl.BlockSpec(w.shape, lambda i: (0, 0))],
        out_specs=row(d),
        out_shape=jax.ShapeDtypeStruct((m, d), F32),
        compiler_params=_cparams(("parallel",)),
        name="out_proj",
    )(yc, ys, ya, h, w)


def _mlp_kernel(h_ref, g_ref, wu_ref, wd_ref, o_ref, xn_ref, acc_ref):
    f = pl.program_id(1)

    @pl.when(f == 0)
    def _():
        xn_ref[...] = _rms(h_ref[...], g_ref[...]).astype(BF16)
        acc_ref[...] = jnp.zeros_like(acc_ref)

    hid = jnp.square(jnp.maximum(_dot(xn_ref[...], wu_ref[...]), 0.0)).astype(BF16)
    acc_ref[...] += _dot(hid, wd_ref[...])

    @pl.when(f == pl.num_programs(1) - 1)
    def _():
        o_ref[...] = h_ref[...] + acc_ref[...]


def mlp(h, g, wu, wd, tm=ROW_TILE, tf=FF_TILE):
    m, d = h.shape
    ff = wu.shape[1]
    return pl.pallas_call(
        _mlp_kernel,
        grid=(m // tm, ff // tf),
        in_specs=[pl.BlockSpec((tm, d), lambda i, f: (i, 0)),
                  pl.BlockSpec((1, d), lambda i, f: (0, 0)),
                  pl.BlockSpec((d, tf), lambda i, f: (0, f)),
                  pl.BlockSpec((tf, d), lambda i, f: (f, 0))],
        out_specs=pl.BlockSpec((tm, d), lambda i, f: (i, 0)),
        out_shape=jax.ShapeDtypeStruct((m, d), F32),
        scratch_shapes=[pltpu.VMEM((tm, d), BF16), pltpu.VMEM((tm, d), F32)],
        compiler_params=_cparams(("parallel", "arbitrary")),
        name="mlp",
    )(h, g, wu, wd)


def _ple_kernel(h_ref, p_ref, g_ref, wg_ref, wp_ref, gf_ref, o_ref, *, final):
    h = h_ref[...]
    gate = jax.nn.sigmoid(_dot(_rms(h, g_ref[...]).astype(BF16), wg_ref[...]))
    out = h + _dot(p_ref[...].astype(BF16), wp_ref[...]) * gate
    if final:
        out = _rms(out, gf_ref[...])
    o_ref[...] = out


def ple(h, p, g, wg, wp, gf, final, tm=ROW_TILE):
    m, d = h.shape
    vec = pl.BlockSpec((1, d), lambda i: (0, 0))
    return pl.pallas_call(
        functools.partial(_ple_kernel, final=final),
        grid=(m // tm,),
        in_specs=[pl.BlockSpec((tm, d), lambda i: (i, 0)),
                  pl.BlockSpec((tm, PLE_DIM), lambda i: (i, 0)),
                  vec,
                  pl.BlockSpec((d, d), lambda i: (0, 0)),
                  pl.BlockSpec((PLE_DIM, d), lambda i: (0, 0)),
                  vec],
        out_specs=pl.BlockSpec((tm, d), lambda i: (i, 0)),
        out_shape=jax.ShapeDtypeStruct((m, d), F32),
        compiler_params=_cparams(("parallel",)),
        name="ple_final" if final else "ple",
    )(h, p, g, wg, wp, gf)


def _expand_matrix(direction):
    e = np.zeros((LANES, SSD_WIDTH), np.float32)
    for hd in range(SSD_HEADS):
        e[direction * SSD_HEADS + hd, hd * SSD_HEAD_DIM:(hd + 1) * SSD_HEAD_DIM] = 1.0
    return jnp.asarray(e)


def _pad_lanes(v):
    return jnp.pad(v.reshape(1, -1), ((0, 0), (0, LANES - v.size)))


def _layer_params(i, norm_mix_g, w_in, conv_w, conv_b, conv_norm_g, conv_norm_b, ssd_conv_w,
                  ssd_conv_b, ssd_dt_bias, ssd_a_log, ssd_d, ssd_norm_g, lambda_q1, lambda_k1,
                  lambda_q2, lambda_k2, attn_subln_g, w_out, norm_mlp_g, w_up, w_down,
                  norm_ple_g, w_ple, w_ple_gate):
    row = lambda v: v.reshape(1, -1)
    w = w_in[i]
    c_conv = 2 * CONV_CH
    c_dt = c_conv + SSD_WIDTH + SSD_XBC
    c_att = c_dt + 2 * SSD_HEADS
    w_ssd = jnp.pad(w[:, c_conv:c_att], ((0, 0), (0, SSD_DT_COLS - 2 * SSD_HEADS)))
    return dict(
        norm_mix_g=row(norm_mix_g[i]),
        w_conv=w[:, :c_conv].astype(BF16),
        w_ssd=w_ssd.astype(BF16),
        w_att=jnp.concatenate([w[:, c_att:c_att + ATT_WIDTH] * (LOG2E * ATT_HEAD_DIM ** -0.5),
                               w[:, c_att + ATT_WIDTH:]], axis=1).astype(BF16),
        conv_w=conv_w[i], conv_b=row(conv_b[i]), conv_g=row(conv_norm_g[i]),
        conv_beta=row(conv_norm_b[i]),
        ssd_cw=ssd_conv_w[i], ssd_cb=row(ssd_conv_b[i]),
        ssd_dtb=_pad_lanes(ssd_dt_bias[i]), ssd_alog=_pad_lanes(ssd_a_log[i]),
        ssd_dskip=row(jnp.repeat(ssd_d[i], SSD_HEAD_DIM)), ssd_ng=row(ssd_norm_g[i]),
        lam=jnp.stack([lambda_q1[i], lambda_k1[i], lambda_q2[i], lambda_k2[i]]),
        subln_g=row(attn_subln_g[i]),
        w_out=w_out[i].astype(BF16),
        norm_mlp_g=row(norm_mlp_g[i]),
        w_up=w_up[i].astype(BF16), w_down=w_down[i].astype(BF16),
        norm_ple_g=row(norm_ple_g[i]),
        w_ple=w_ple[i].astype(BF16), w_gate=w_ple_gate[i].astype(BF16),
    )


def _run(x, p, layers, bias, expand_f, expand_b, final_g):
    bsz, s, d = x.shape
    m = bsz * s
    h = x.reshape(m, d)
    for i, lp in enumerate(layers):
        lam_init = 0.8 - 0.6 * math.exp(-0.3 * i)
        ps, pa, y_conv = proj_conv(h, lp["norm_mix_g"], lp["w_conv"], lp["w_ssd"], lp["w_att"],
                                   lp["conv_w"], lp["conv_b"], lp["conv_g"], lp["conv_beta"], s)
        ps = ps.reshape(bsz, s, SSD_SLAB)
        pa = pa.reshape(bsz, s, ATT_SLAB)
        y_ssd = ssd_mixer(ps, lp["ssd_cw"], lp["ssd_cb"], lp["ssd_dtb"], lp["ssd_alog"],
                          expand_f, expand_b, lp["ssd_dskip"], lp["ssd_ng"])
        y_att = diff_attention(pa, bias, lp["lam"], lp["subln_g"], lam_init)
        h = out_proj(y_conv, y_ssd.reshape(m, SSD_WIDTH),
                     y_att.reshape(m, ATT_WIDTH), h, lp["w_out"])
        h = mlp(h, lp["norm_mlp_g"], lp["w_up"], lp["w_down"])
        h = ple(h, p[i].reshape(m, PLE_DIM), lp["norm_ple_g"], lp["w_gate"], lp["w_ple"],
                final_g, final=(i == len(layers) - 1))
    return h.reshape(bsz, s, d)


def kernel(x_prompt, x_sample, p_prompt, p_sample, norm_mix_g, w_in, conv_w, conv_b, conv_norm_g,
           conv_norm_b, ssd_conv_w, ssd_conv_b, ssd_dt_bias, ssd_a_log, ssd_d, ssd_norm_g,
           lambda_q1, lambda_k1, lambda_q2, lambda_k2, attn_subln_g, rel_bias, w_out, norm_mlp_g,
           w_up, w_down, norm_ple_g, w_ple, w_ple_gate, final_norm_g):
    layers = [_layer_params(i, norm_mix_g, w_in, conv_w, conv_b, conv_norm_g, conv_norm_b,
                            ssd_conv_w, ssd_conv_b, ssd_dt_bias, ssd_a_log, ssd_d, ssd_norm_g,
                            lambda_q1, lambda_k1, lambda_q2, lambda_k2, attn_subln_g, w_out,
                            norm_mlp_g, w_up, w_down, norm_ple_g, w_ple, w_ple_gate)
              for i in range(DEPTH)]
    bias = bias_tiles(_bucket_tiles(ATT_TILE), rel_bias)
    expand_f, expand_b = _expand_matrix(0), _expand_matrix(1)
    final_g = final_norm_g.reshape(1, -1)
    y_prompt = _run(x_prompt, p_prompt, layers, bias, expand_f, expand_b, final_g)
    y_sample = _run(x_sample, p_sample, layers, bias, expand_f, expand_b, final_g)
    return (y_prompt, y_sample)
```

```python
import functools
import math

import jax
import jax.numpy as jnp
import numpy as np
from jax import lax
from jax.experimental import pallas as pl
from jax.experimental.pallas import tpu as pltpu

F32 = jnp.float32
BF16 = jnp.bfloat16

D_MODEL = 2048
DEPTH = 2
PLE_DIM = 256
EPS = 1e-6
CONV_CH = 512
CONV_WIDTH = 31
CONV_PAD = (CONV_WIDTH - 1) // 2
SSD_HEAD_DIM = 64
SSD_HEADS = 12
SSD_WIDTH = SSD_HEADS * SSD_HEAD_DIM
SSD_GROUPS = 2
SSD_HPG = SSD_HEADS // SSD_GROUPS
SSD_STATE = 128
SSD_CONV_WIDTH = 5
SSD_CONV_PAD = (SSD_CONV_WIDTH - 1) // 2
SSD_CHUNK = 128
SSD_XBC = SSD_WIDTH + 2 * SSD_GROUPS * SSD_STATE
ATT_HEADS = 6
ATT_HEAD_DIM = 64
ATT_WIDTH = ATT_HEADS * 2 * ATT_HEAD_DIM
REL_BUCKETS = 32
REL_MAX_DIST = 128
D_FF = 4 * D_MODEL
LOG2E = math.log2(math.e)

LANES = 128
SUBLANES = 8
VMEM_LIMIT = 56 * 1024 * 1024

SSD_DT_COLS = LANES
SSD_SLAB = SSD_WIDTH + SSD_XBC + SSD_DT_COLS
CONV_SLAB = 2 * CONV_CH
ATT_SLAB = 3 * ATT_WIDTH
SSD_GW = SSD_HPG * SSD_HEAD_DIM

ATT_TILE = 512
ATT_QTILES = 2
N_BIAS_TILES = 5
ATT_ROWS = 32
CONV_TILE = 128
CONV_HALO = 16
CONV_ROWS = 32
SSD_HALO = SUBLANES
SSD_BATCH = 4
ROW_TILE = 512
FF_TILE = 1024
CAST_BLOCK_BYTES = 8 * 1024 * 1024


def _cparams(sem):
    return pltpu.CompilerParams(dimension_semantics=sem, vmem_limit_bytes=VMEM_LIMIT)


def _rms(x, g):
    ms = jnp.mean(x * x, axis=-1, keepdims=True)
    return (x * lax.rsqrt(ms + EPS)) * g


def _silu(x):
    return x * jax.nn.sigmoid(x)


def _dot(a, b):
    return jnp.dot(a, b, preferred_element_type=F32)


def _split3(x):
    hi = x.astype(BF16)
    r1 = x - hi.astype(F32)
    mid = r1.astype(BF16)
    lo = (r1 - mid.astype(F32)).astype(BF16)
    return hi, mid, lo


def _select_dot(sel, x):
    sel = sel.astype(BF16)
    hi, mid, lo = _split3(x)
    return _dot(sel, hi) + (_dot(sel, mid) + _dot(sel, lo))


def _dot_select(x, sel):
    sel = sel.astype(BF16)
    hi = x.astype(BF16)
    mid = (x - hi.astype(F32)).astype(BF16)
    return _dot(hi, sel) + _dot(mid, sel)


def _dot_nt(a, b):
    return lax.dot_general(a, b, (((1,), (1,)), ((), ())), preferred_element_type=F32)


def _conv_tile(buf_ref, shift_ref, w, b_ref, g_ref, beta_ref, o_ref, row0):
    span = CONV_TILE + 2 * CONV_HALO - SUBLANES
    for b in range(1, SUBLANES):
        shift_ref[b - 1] = buf_ref[b:b + span, :]
    for r0 in range(0, CONV_TILE, CONV_ROWS):
        acc = jnp.broadcast_to(b_ref[...], (CONV_ROWS, CONV_CH))
        for k in range(CONV_WIDTH):
            start = r0 + CONV_HALO - CONV_PAD + k
            b, base = start % SUBLANES, start - start % SUBLANES
            window = (buf_ref[base:base + CONV_ROWS, :] if b == 0
                      else shift_ref[b - 1, base:base + CONV_ROWS, :])
            acc = acc + w[k:k + 1, :] * window
        mu = jnp.mean(acc, axis=-1, keepdims=True)
        xc = acc - mu
        var = jnp.mean(xc * xc, axis=-1, keepdims=True)
        y = xc * lax.rsqrt(var + EPS) * g_ref[...] + beta_ref[...]
        o_ref[row0 + r0:row0 + r0 + CONV_ROWS, :] = _silu(y).astype(o_ref.dtype)


def _proj_conv_kernel(x_ref, g_ref, wc_ref, ws_ref, wa_ref, cw_ref, cb_ref, cg_ref, cbeta_ref,
                      os_ref, oa_ref, oy_ref, ubuf_ref, unext_ref, buf_ref, shift_ref,
                      *, tiles_per_seq):
    tm, halo = ROW_TILE, CONV_HALO
    i = pl.program_id(0)

    @pl.when(i == 0)
    def _():
        ubuf_ref[...] = jnp.zeros_like(ubuf_ref)
        unext_ref[...] = jnp.zeros_like(unext_ref)

    ubuf_ref[0:halo, :] = ubuf_ref[tm:tm + halo, :]
    ubuf_ref[halo:halo + tm, :] = unext_ref[...]

    xn = _rms(x_ref[...], g_ref[...]).astype(BF16)
    pc = _dot(xn, wc_ref[...])
    u = pc[:, :CONV_CH] * jax.nn.sigmoid(pc[:, CONV_CH:])
    unext_ref[...] = u
    ubuf_ref[halo + tm:, :] = u[0:halo, :]
    os_ref[...] = _dot(xn, ws_ref[...])
    oa_ref[...] = _dot(xn, wa_ref[...]).astype(oa_ref.dtype)

    pos = lax.rem(i + tiles_per_seq - 1, tiles_per_seq)
    w = cw_ref[...]
    n_sub = tm // CONV_TILE
    for st in range(n_sub):
        base = st * CONV_TILE
        buf_ref[...] = ubuf_ref[base:base + CONV_TILE + 2 * halo, :]
        if st == 0:
            buf_ref[0:halo, :] = jnp.where(pos > 0, ubuf_ref[0:halo, :], 0.0)
        if st == n_sub - 1:
            buf_ref[halo + CONV_TILE:, :] = jnp.where(pos < tiles_per_seq - 1,
                                                      ubuf_ref[halo + tm:, :], 0.0)
        _conv_tile(buf_ref, shift_ref, w, cb_ref, cg_ref, cbeta_ref, oy_ref, base)


def proj_conv(x, g, w_conv, w_ssd, w_att, cw, cb, cg, cbeta, seq_len, tm=ROW_TILE):
    m, d = x.shape
    assert seq_len % tm == 0 and m % seq_len == 0
    n = m // tm
    resident = lambda w: pl.BlockSpec(w.shape, lambda i: (0, 0), pipeline_mode=pl.Buffered(1))
    cur = lambda cols: pl.BlockSpec((tm, cols), lambda i: (jnp.minimum(i, n - 1), 0))
    vec = lambda: pl.BlockSpec((1, CONV_CH), lambda i: (0, 0))
    return pl.pallas_call(
        functools.partial(_proj_conv_kernel, tiles_per_seq=seq_len // tm),
        grid=(n + 1,),
        in_specs=[cur(d), pl.BlockSpec((1, d), lambda i: (0, 0)),
                  resident(w_conv), resident(w_ssd), resident(w_att),
                  pl.BlockSpec((CONV_WIDTH, CONV_CH), lambda i: (0, 0)), vec(), vec(), vec()],
        out_specs=[cur(SSD_SLAB), cur(ATT_SLAB),
                   pl.BlockSpec((tm, CONV_CH), lambda i: (jnp.maximum(i - 1, 0), 0))],
        out_shape=[jax.ShapeDtypeStruct((m, SSD_SLAB), F32),
                   jax.ShapeDtypeStruct((m, ATT_SLAB), BF16),
                   jax.ShapeDtypeStruct((m, CONV_CH), BF16)],
        scratch_shapes=[pltpu.VMEM((tm + 2 * CONV_HALO, CONV_CH), F32),
                        pltpu.VMEM((tm, CONV_CH), F32),
                        pltpu.VMEM((CONV_TILE + 2 * CONV_HALO, CONV_CH), F32),
                        pltpu.VMEM((SUBLANES - 1, CONV_TILE + 2 * CONV_HALO - SUBLANES, CONV_CH),
                                   F32)],
        compiler_params=_cparams(("arbitrary",)),
        name="proj_conv",
    )(x, g, w_conv, w_ssd, w_att, cw, cb, cg, cbeta)


def _ssd_chunk(xbc, dt_raw, dtb, alog, expand, state_ref, bi, *, reverse):
    q = SSD_CHUNK
    xs = xbc[:, :SSD_WIDTH]
    bmat = xbc[:, SSD_WIDTH:SSD_WIDTH + SSD_GROUPS * SSD_STATE]
    cmat = xbc[:, SSD_WIDTH + SSD_GROUPS * SSD_STATE:]

    dt = jax.nn.softplus(dt_raw + dtb)
    dta = dt * (-jnp.exp(alog))
    row = lax.broadcasted_iota(jnp.int32, (q, q), 0)
    col = lax.broadcasted_iota(jnp.int32, (q, q), 1)
    keep = (row <= col) if reverse else (row >= col)
    a_cs = _select_dot(keep, dta)
    a_tot = jnp.sum(dta, axis=0, keepdims=True)
    ea = jnp.exp(a_cs)
    dec = jnp.exp(a_tot - a_cs)
    cdec = jnp.broadcast_to(jnp.exp(a_tot), (SUBLANES, LANES))
    ex = _dot_select(jnp.concatenate([dt, ea, dec, cdec], axis=0), expand)
    dtx, eax, decx, cdecx = ex[0:q], ex[q:2 * q], ex[2 * q:3 * q], ex[3 * q:3 * q + 1]
    xdt = xs * dtx
    xdt_b = xdt.astype(BF16)
    xd_b = (xdt * decx).astype(BF16)
    a_cs_t = a_cs.T
    lane = lax.broadcasted_iota(jnp.int32, (q, LANES), 1)
    lo_half = lane < SSD_HEAD_DIM
    lane0 = SSD_HEADS if reverse else 0

    y_groups = []
    for g in range(SSD_GROUPS):
        gs = slice(g * SSD_GW, (g + 1) * SSD_GW)
        bg = bmat[:, g * SSD_STATE:(g + 1) * SSD_STATE]
        cg_b = cmat[:, g * SSD_STATE:(g + 1) * SSD_STATE].astype(BF16)
        cb = _dot_nt(cg_b, bg.astype(BF16))
        prev = state_ref[bi, :, gs]
        y_off = _dot(cg_b, prev.astype(BF16)) * eax[:, gs]
        new_states = _dot(bg.T.astype(BF16), xd_b[:, gs])
        state_ref[bi, :, gs] = prev * cdecx[:, gs] + new_states
        pairs = []
        for j in range(SSD_HPG // 2):
            xpair = xdt_b[:, g * SSD_GW + j * LANES:g * SSD_GW + (j + 1) * LANES]
            ypair = None
            for half in range(2):
                ln = lane0 + g * SSD_HPG + 2 * j + half
                seg = a_cs[:, ln:ln + 1] - a_cs_t[ln:ln + 1, :]
                decay = jnp.exp(jnp.where(keep, seg, -jnp.inf))
                mixer = (cb * decay).astype(BF16)
                xh = jnp.where(lo_half if half == 0 else jnp.logical_not(lo_half), xpair,
                               jnp.zeros_like(xpair))
                part = _dot(mixer, xh)
                ypair = part if ypair is None else ypair + part
            pairs.append(ypair)
        y_groups.append(jnp.concatenate(pairs, axis=1) + y_off)
    return jnp.concatenate(y_groups, axis=1), xs


def _ssd_fwd_kernel(cur_ref, left_ref, right_ref, cw_ref, cb_ref, dtb_ref, alog_ref, exp_ref,
                    y_ref, xc_ref, state_ref, buf_ref):
    q = SSD_CHUNK
    c = pl.program_id(1)
    nc = pl.num_programs(1)

    @pl.when(c == 0)
    def _():
        state_ref[...] = jnp.zeros_like(state_ref)

    xbc_lo, xbc_hi = SSD_WIDTH, SSD_WIDTH + SSD_XBC
    cw = cw_ref[...]
    for bi in range(SSD_BATCH):
        blk = cur_ref[bi]
        buf_ref[bi, 0:SSD_HALO, :] = jnp.where(c > 0, left_ref[bi][:, xbc_lo:xbc_hi], 0.0)
        buf_ref[bi, SSD_HALO:SSD_HALO + q, :] = blk[:, xbc_lo:xbc_hi]
        buf_ref[bi, SSD_HALO + q:, :] = jnp.where(c < nc - 1, right_ref[bi][:, xbc_lo:xbc_hi], 0.0)
        acc = jnp.broadcast_to(cb_ref[...], (q, SSD_XBC))
        for k in range(SSD_CONV_WIDTH):
            start = SSD_HALO - SSD_CONV_PAD + k
            acc = acc + cw[k:k + 1, :] * buf_ref[bi, start:start + q, :]
        xbc = _silu(acc)
        xc_ref[bi] = xbc
        y, _ = _ssd_chunk(xbc, blk[:, xbc_hi:], dtb_ref[...], alog_ref[...], exp_ref[...],
                          state_ref, bi, reverse=False)
        y_ref[bi] = y


def _ssd_bwd_kernel(xc_ref, dt_ref, z_ref, yf_ref, dtb_ref, alog_ref, exp_ref, dskip_ref, ng_ref,
                    o_ref, state_ref):
    c = pl.program_id(1)

    @pl.when(c == 0)
    def _():
        state_ref[...] = jnp.zeros_like(state_ref)

    for bi in range(SSD_BATCH):
        y, xs = _ssd_chunk(xc_ref[bi], dt_ref[bi], dtb_ref[...], alog_ref[...], exp_ref[...],
                           state_ref, bi, reverse=True)
        y = y + yf_ref[bi] + dskip_ref[...] * xs
        o_ref[bi] = _rms(y * _silu(z_ref[bi]), ng_ref[...]).astype(o_ref.dtype)


def ssd_mixer(ps, cw, cb, dtb, alog, expand_f, expand_b, dskip, ng):
    bsz, s, _ = ps.shape
    q, nb = SSD_CHUNK, SSD_BATCH
    assert bsz % nb == 0 and s % q == 0
    nc = s // q
    hb = q // SSD_HALO
    last_halo = s // SSD_HALO - 1
    full = lambda shape: pl.BlockSpec(shape, lambda i, c: (0,) * len(shape))
    params = [full((1, LANES)), full((1, LANES)), full((LANES, SSD_WIDTH))]
    state = pltpu.VMEM((nb, SSD_STATE, SSD_WIDTH), F32)
    fwd = lambda width, col=0: pl.BlockSpec((nb, q, width), lambda i, c: (i, c, col))
    y_f, xc = pl.pallas_call(
        _ssd_fwd_kernel,
        grid=(bsz // nb, nc),
        in_specs=[fwd(SSD_SLAB),
                  pl.BlockSpec((nb, SSD_HALO, SSD_SLAB),
                               lambda i, c: (i, jnp.maximum(c * hb - 1, 0), 0)),
                  pl.BlockSpec((nb, SSD_HALO, SSD_SLAB),
                               lambda i, c: (i, jnp.minimum((c + 1) * hb, last_halo), 0)),
                  full((SSD_CONV_WIDTH, SSD_XBC)), full((1, SSD_XBC))] + params,
        out_specs=[fwd(SSD_WIDTH), fwd(SSD_XBC)],
        out_shape=[jax.ShapeDtypeStruct((bsz, s, SSD_WIDTH), F32),
                   jax.ShapeDtypeStruct((bsz, s, SSD_XBC), F32)],
        scratch_shapes=[state, pltpu.VMEM((nb, q + 2 * SSD_HALO, SSD_XBC), F32)],
        compiler_params=_cparams(("parallel", "arbitrary")),
        name="ssd_fwd",
    )(ps, ps, ps, cw, cb, dtb, alog, expand_f)
    bwd = lambda width, col=0: pl.BlockSpec((nb, q, width), lambda i, c: (i, nc - 1 - c, col))
    dt_col = (SSD_WIDTH + SSD_XBC) // SSD_DT_COLS
    return pl.pallas_call(
        _ssd_bwd_kernel,
        grid=(bsz // nb, nc),
        in_specs=[bwd(SSD_XBC), bwd(SSD_DT_COLS, dt_col), bwd(SSD_WIDTH), bwd(SSD_WIDTH)]
                 + params + [full((1, SSD_WIDTH)), full((1, SSD_WIDTH))],
        out_specs=bwd(SSD_WIDTH),
        out_shape=jax.ShapeDtypeStruct((bsz, s, SSD_WIDTH), BF16),
        scratch_shapes=[state],
        compiler_params=_cparams(("parallel", "arbitrary")),
        name="ssd_bwd",
    )(xc, ps, ps, y_f, dtb, alog, expand_b, dskip, ng)


def _bias_tiles_kernel(bucket_ref, table_ref, o_ref):
    h = pl.program_id(0)
    bucket = bucket_ref[0]
    out = jnp.zeros(bucket.shape, F32)
    for b in range(REL_BUCKETS):
        out = jnp.where(bucket == b, table_ref[b, h] * LOG2E, out)
    o_ref[0, 0] = out


def bias_tiles(buckets, rel_bias):
    nt, t, _ = buckets.shape
    return pl.pallas_call(
        _bias_tiles_kernel,
        grid=(ATT_HEADS, nt),
        in_specs=[pl.BlockSpec((1, t, t), lambda h, d: (d, 0, 0)),
                  pl.BlockSpec(memory_space=pltpu.SMEM)],
        out_specs=pl.BlockSpec((1, 1, t, t), lambda h, d: (h, d, 0, 0)),
        out_shape=jax.ShapeDtypeStruct((ATT_HEADS, nt, t, t), F32),
        compiler_params=_cparams(("parallel", "parallel")),
        name="bias_tiles",
    )(buckets, rel_bias)


def _rel_bucket(rel):
    nb = REL_BUCKETS // 2
    max_exact = nb // 2
    ret = jnp.where(rel > 0, nb, 0)
    n = jnp.abs(rel)
    nf = jnp.maximum(n, 1).astype(jnp.float32)
    large = max_exact + (jnp.log(nf / max_exact) / math.log(REL_MAX_DIST / max_exact)
                         * (nb - max_exact)).astype(jnp.int32)
    large = jnp.minimum(large, nb - 1)
    return ret + jnp.where(n < max_exact, n, large)


def _bucket_tiles(t):
    i = jnp.arange(t, dtype=jnp.int32)
    half = N_BIAS_TILES // 2
    offs = (jnp.arange(N_BIAS_TILES, dtype=jnp.int32) - half) * t
    rel = offs[:, None, None] + i[None, None, :] - i[None, :, None]
    return _rel_bucket(rel).astype(jnp.int32)


def _attn_kernel(q_ref, k_ref, v_ref, bias_ref, lam_ref, g_ref, o_ref,
                 qz_ref, s_ref, p_ref, m_ref, l_ref, alpha_ref, acc_ref, *, lam_init, nk):
    t = ATT_TILE
    tq = ATT_QTILES * t
    qi = pl.program_id(2)
    qb = q_ref[0]
    lane = lax.broadcasted_iota(jnp.int32, (tq, 2 * ATT_HEAD_DIM), 1)
    qz_ref[0] = jnp.where(lane < ATT_HEAD_DIM, qb, jnp.zeros_like(qb))
    qz_ref[1] = jnp.where(lane >= ATT_HEAD_DIM, qb, jnp.zeros_like(qb))
    m_ref[...] = jnp.full(m_ref.shape, -jnp.inf, F32)
    l_ref[...] = jnp.zeros_like(l_ref)
    acc_ref[...] = jnp.zeros_like(acc_ref)
    half = N_BIAS_TILES // 2
    rb = ATT_ROWS

    def key_rows(kj):
        start = kj * t
        return pl.ds(start if isinstance(start, int) else pl.multiple_of(start, t), t)

    def apply_pv(slot, kj):
        vb = v_ref[0, key_rows(kj), :]
        for mp in range(2):
            acc_ref[mp] = alpha_ref[slot, mp] * acc_ref[mp] + _dot(p_ref[slot, mp], vb)

    def logits(slot, kj):
        kb = k_ref[0, key_rows(kj), :]
        for mp in range(2):
            s_ref[slot, mp] = _dot_nt(qz_ref[mp], kb)

    def softmax(slot, kj):
        tiles = [jnp.clip(kj - (ATT_QTILES * qi + u), -half, half) + half
                 for u in range(ATT_QTILES)]
        for mp in range(2):
            for r0 in range(0, tq, rb):
                rows = slice(r0, r0 + rb)
                bias_rows = slice(r0 % t, r0 % t + rb)
                s = s_ref[slot, mp, rows, :] + bias_ref[0, tiles[r0 // t], bias_rows, :]
                m_old = m_ref[mp, rows, :]
                m_new = jnp.maximum(m_old, jnp.max(s, axis=-1, keepdims=True))
                alpha = jnp.exp2(m_old - m_new)
                p = jnp.exp2(s - jnp.concatenate([m_new] * (t // LANES), axis=1))
                l_ref[mp, rows, :] = (alpha * l_ref[mp, rows, :]
                                      + jnp.sum(p, axis=-1, keepdims=True))
                m_ref[mp, rows, :] = m_new
                alpha_ref[slot, mp, rows, :] = alpha
                p_ref[slot, mp, rows, :] = p.astype(BF16)

    def pair(kk, carry, first=False):
        for slot in range(2):
            kj = 2 * kk + slot
            logits(slot, kj)
            if not (first and slot == 0):
                apply_pv(1 - slot, kj - 1)
            softmax(slot, kj)
        return carry

    pair(0, 0, first=True)
    lax.fori_loop(1, nk // 2, pair, 0)
    apply_pv(1, nk - 1)

    lp = lam_ref[...]
    lam = (jnp.exp(jnp.sum(lp[0:1] * lp[1:2], axis=-1, keepdims=True))
           - jnp.exp(jnp.sum(lp[2:3] * lp[3:4], axis=-1, keepdims=True)) + lam_init)
    o = acc_ref[0] / l_ref[0] - lam * (acc_ref[1] / l_ref[1])
    o_ref[0] = (_rms(o, g_ref[...]) * (1.0 - lam_init)).astype(o_ref.dtype)


def diff_attention(qkv, bias, lam_params, subln_g, lam_init):
    bsz, s, _ = qkv.shape
    t = ATT_TILE
    tq = ATT_QTILES * t
    nk = s // t
    hw = 2 * ATT_HEAD_DIM
    assert nk % 2 == 0 and s % tq == 0
    stat = pltpu.VMEM((2, tq, LANES), F32)
    return pl.pallas_call(
        functools.partial(_attn_kernel, lam_init=lam_init, nk=nk),
        grid=(bsz, ATT_HEADS, s // tq),
        in_specs=[
            pl.BlockSpec((1, tq, hw), lambda b, h, i: (b, i, h)),
            pl.BlockSpec((1, s, hw), lambda b, h, i: (b, 0, ATT_HEADS + h)),
            pl.BlockSpec((1, s, hw), lambda b, h, i: (b, 0, 2 * ATT_HEADS + h)),
            pl.BlockSpec((1, N_BIAS_TILES, t, t), lambda b, h, i: (h, 0, 0, 0)),
            pl.BlockSpec((4, ATT_HEAD_DIM), lambda b, h, i: (0, 0)),
            pl.BlockSpec((1, hw), lambda b, h, i: (0, 0)),
        ],
        out_specs=pl.BlockSpec((1, tq, hw), lambda b, h, i: (b, i, h)),
        out_shape=jax.ShapeDtypeStruct((bsz, s, ATT_WIDTH), BF16),
        scratch_shapes=[pltpu.VMEM((2, tq, hw), BF16),
                        pltpu.VMEM((2, 2, tq, t), F32),
                        pltpu.VMEM((2, 2, tq, t), BF16),
                        stat, stat,
                        pltpu.VMEM((2, 2, tq, LANES), F32),
                        pltpu.VMEM((2, tq, hw), F32)],
        compiler_params=_cparams(("parallel", "parallel", "parallel")),
        name="diff_attention",
    )(qkv, qkv, qkv, bias, lam_params, subln_g)


def _out_proj_kernel(yc_ref, ys_ref, ya_ref, h_ref, w_ref, o_ref):
    c0, c1 = CONV_CH, CONV_CH + SSD_WIDTH
    acc = _dot(yc_ref[...], w_ref[0:c0, :])
    acc = acc + _dot(ys_ref[...], w_ref[c0:c1, :])
    acc = acc + _dot(ya_ref[...], w_ref[c1:, :])
    o_ref[...] = h_ref[...] + acc


def out_proj(yc, ys, ya, h, w, tm=ROW_TILE):
    m, d = h.shape
    row = lambda n: pl.BlockSpec((tm, n), lambda i: (i, 0))
    return pl.pallas_call(
        _out_proj_kernel,
        grid=(m // tm,),
        in_specs=[row(CONV_CH), row(SSD_WIDTH), row(ATT_WIDTH), row(d),
                  pl.BlockSpec(w.shape, lambda i: (0, 0))],
        out_specs=row(d),
        out_shape=jax.ShapeDtypeStruct((m, d), F32),
        compiler_params=_cparams(("parallel",)),
        name="out_proj",
    )(yc, ys, ya, h, w)


def _mlp_kernel(h_ref, g_ref, wu_ref, wd_ref, o_ref, xn_ref, acc_ref):
    f = pl.program_id(1)

    @pl.when(f == 0)
    def _():
        xn_ref[...] = _rms(h_ref[...], g_ref[...]).astype(BF16)
        acc_ref[...] = jnp.zeros_like(acc_ref)

    hid = jnp.square(jnp.maximum(_dot(xn_ref[...], wu_ref[...]), 0.0)).astype(BF16)
    acc_ref[...] += _dot(hid, wd_ref[...])

    @pl.when(f == pl.num_programs(1) - 1)
    def _():
        o_ref[...] = h_ref[...] + acc_ref[...]


def mlp(h, g, wu, wd, tm=ROW_TILE, tf=FF_TILE):
    m, d = h.shape
    ff = wu.shape[1]
    return pl.pallas_call(
        _mlp_kernel,
        grid=(m // tm, ff // tf),
        in_specs=[pl.BlockSpec((tm, d), lambda i, f: (i, 0)),
                  pl.BlockSpec((1, d), lambda i, f: (0, 0)),
                  pl.BlockSpec((d, tf), lambda i, f: (0, f)),
                  pl.BlockSpec((tf, d), lambda i, f: (f, 0))],
        out_specs=pl.BlockSpec((tm, d), lambda i, f: (i, 0)),
        out_shape=jax.ShapeDtypeStruct((m, d), F32),
        scratch_shapes=[pltpu.VMEM((tm, d), BF16), pltpu.VMEM((tm, d), F32)],
        compiler_params=_cparams(("parallel", "arbitrary")),
        name="mlp",
    )(h, g, wu, wd)


def _ple_kernel(h_ref, p_ref, g_ref, wg_ref, wp_ref, gf_ref, o_ref, *, final):
    h = h_ref[...]
    gate = jax.nn.sigmoid(_dot(_rms(h, g_ref[...]).astype(BF16), wg_ref[...]))
    out = h + _dot(p_ref[...].astype(BF16), wp_ref[...]) * gate
    if final:
        out = _rms(out, gf_ref[...])
    o_ref[...] = out


def ple(h, p, g, wg, wp, gf, final, tm=ROW_TILE):
    m, d = h.shape
    vec = pl.BlockSpec((1, d), lambda i: (0, 0))
    return pl.pallas_call(
        functools.partial(_ple_kernel, final=final),
        grid=(m // tm,),
        in_specs=[pl.BlockSpec((tm, d), lambda i: (i, 0)),
                  pl.BlockSpec((tm, PLE_DIM), lambda i: (i, 0)),
                  vec,
                  pl.BlockSpec((d, d), lambda i: (0, 0)),
                  pl.BlockSpec((PLE_DIM, d), lambda i: (0, 0)),
                  vec],
        out_specs=pl.BlockSpec((tm, d), lambda i: (i, 0)),
        out_shape=jax.ShapeDtypeStruct((m, d), F32),
        compiler_params=_cparams(("parallel",)),
        name="ple_final" if final else "ple",
    )(h, p, g, wg, wp, gf)


def _cast_kernel(w_ref, o_ref):
    o_ref[...] = w_ref[0].astype(o_ref.dtype)


def cast_layer(w, layer):
    _, r, c = w.shape
    tr = min(r, CAST_BLOCK_BYTES // (4 * c))
    assert r % tr == 0 and tr % SUBLANES == 0
    return pl.pallas_call(
        _cast_kernel,
        grid=(r // tr,),
        in_specs=[pl.BlockSpec((1, tr, c), lambda j: (layer, j, 0))],
        out_specs=pl.BlockSpec((tr, c), lambda j: (j, 0)),
        out_shape=jax.ShapeDtypeStruct((r, c), BF16),
        compiler_params=_cparams(("parallel",)),
        name="cast_layer",
    )(w)


def _expand_matrix(direction):
    e = np.zeros((LANES, SSD_WIDTH), np.float32)
    for hd in range(SSD_HEADS):
        e[direction * SSD_HEADS + hd, hd * SSD_HEAD_DIM:(hd + 1) * SSD_HEAD_DIM] = 1.0
    return jnp.asarray(e)


def _pad_lanes(v):
    return jnp.pad(v.reshape(1, -1), ((0, 0), (0, LANES - v.size)))


def _layer_params(i, norm_mix_g, w_in, conv_w, conv_b, conv_norm_g, conv_norm_b, ssd_conv_w,
                  ssd_conv_b, ssd_dt_bias, ssd_a_log, ssd_d, ssd_norm_g, lambda_q1, lambda_k1,
                  lambda_q2, lambda_k2, attn_subln_g, w_out, norm_mlp_g, w_up, w_down,
                  norm_ple_g, w_ple, w_ple_gate):
    row = lambda v: v.reshape(1, -1)
    w = w_in[i]
    c_conv = 2 * CONV_CH
    c_dt = c_conv + SSD_WIDTH + SSD_XBC
    c_att = c_dt + 2 * SSD_HEADS
    w_ssd = jnp.pad(w[:, c_conv:c_att], ((0, 0), (0, SSD_DT_COLS - 2 * SSD_HEADS)))
    return dict(
        norm_mix_g=row(norm_mix_g[i]),
        w_conv=w[:, :c_conv].astype(BF16),
        w_ssd=w_ssd.astype(BF16),
        w_att=jnp.concatenate([w[:, c_att:c_att + ATT_WIDTH] * (LOG2E * ATT_HEAD_DIM ** -0.5),
                               w[:, c_att + ATT_WIDTH:]], axis=1).astype(BF16),
        conv_w=conv_w[i], conv_b=row(conv_b[i]), conv_g=row(conv_norm_g[i]),
        conv_beta=row(conv_norm_b[i]),
        ssd_cw=ssd_conv_w[i], ssd_cb=row(ssd_conv_b[i]),
        ssd_dtb=_pad_lanes(ssd_dt_bias[i]), ssd_alog=_pad_lanes(ssd_a_log[i]),
        ssd_dskip=row(jnp.repeat(ssd_d[i], SSD_HEAD_DIM)), ssd_ng=row(ssd_norm_g[i]),
        lam=jnp.stack([lambda_q1[i], lambda_k1[i], lambda_q2[i], lambda_k2[i]]),
        subln_g=row(attn_subln_g[i]),
        w_out=cast_layer(w_out, i),
        norm_mlp_g=row(norm_mlp_g[i]),
        w_up=cast_layer(w_up, i), w_down=cast_layer(w_down, i),
        norm_ple_g=row(norm_ple_g[i]),
        w_ple=cast_layer(w_ple, i), w_gate=cast_layer(w_ple_gate, i),
    )


def _run(x, p, layers, bias, expand_f, expand_b, final_g):
    bsz, s, d = x.shape
    m = bsz * s
    h = x.reshape(m, d)
    for i, lp in enumerate(layers):
        lam_init = 0.8 - 0.6 * math.exp(-0.3 * i)
        ps, pa, y_conv = proj_conv(h, lp["norm_mix_g"], lp["w_conv"], lp["w_ssd"], lp["w_att"],
                                   lp["conv_w"], lp["conv_b"], lp["conv_g"], lp["conv_beta"], s)
        ps = ps.reshape(bsz, s, SSD_SLAB)
        pa = pa.reshape(bsz, s, ATT_SLAB)
        y_ssd = ssd_mixer(ps, lp["ssd_cw"], lp["ssd_cb"], lp["ssd_dtb"], lp["ssd_alog"],
                          expand_f, expand_b, lp["ssd_dskip"], lp["ssd_ng"])
        y_att = diff_attention(pa, bias, lp["lam"], lp["subln_g"], lam_init)
        h = out_proj(y_conv, y_ssd.reshape(m, SSD_WIDTH),
                     y_att.reshape(m, ATT_WIDTH), h, lp["w_out"])
        h = mlp(h, lp["norm_mlp_g"], lp["w_up"], lp["w_down"])
        h = ple(h, p[i].reshape(m, PLE_DIM), lp["norm_ple_g"], lp["w_gate"], lp["w_ple"],
                final_g, final=(i == len(layers) - 1))
    return h.reshape(bsz, s, d)


def kernel(x_prompt, x_sample, p_prompt, p_sample, norm_mix_g, w_in, conv_w, conv_b, conv_norm_g,
           conv_norm_b, ssd_conv_w, ssd_conv_b, ssd_dt_bias, ssd_a_log, ssd_d, ssd_norm_g,
           lambda_q1, lambda_k1, lambda_q2, lambda_k2, attn_subln_g, rel_bias, w_out, norm_mlp_g,
           w_up, w_down, norm_ple_g, w_ple, w_ple_gate, final_norm_g):
    layers = [_layer_params(i, norm_mix_g, w_in, conv_w, conv_b, conv_norm_g, conv_norm_b,
                            ssd_conv_w, ssd_conv_b, ssd_dt_bias, ssd_a_log, ssd_d, ssd_norm_g,
                            lambda_q1, lambda_k1, lambda_q2, lambda_k2, attn_subln_g, w_out,
                            norm_mlp_g, w_up, w_down, norm_ple_g, w_ple, w_ple_gate)
              for i in range(DEPTH)]
    bias = bias_tiles(_bucket_tiles(ATT_TILE), rel_bias)
    expand_f, expand_b = _expand_matrix(0), _expand_matrix(1)
    final_g = final_norm_g.reshape(1, -1)
    y_prompt = _run(x_prompt, p_prompt, layers, bias, expand_f, expand_b, final_g)
    y_sample = _run(x_sample, p_sample, layers, bias, expand_f, expand_b, final_g)
    return (y_prompt, y_sample)
```

```python
import functools
import math

import jax
import jax.numpy as jnp
import numpy as np
from jax import lax
from jax.experimental import pallas as pl
from jax.experimental.pallas import tpu as pltpu

F32 = jnp.float32
BF16 = jnp.bfloat16

D_MODEL = 2048
DEPTH = 2
PLE_DIM = 256
EPS = 1e-6
CONV_CH = 512
CONV_WIDTH = 31
CONV_PAD = (CONV_WIDTH - 1) // 2
SSD_HEAD_DIM = 64
SSD_HEADS = 12
SSD_WIDTH = SSD_HEADS * SSD_HEAD_DIM
SSD_GROUPS = 2
SSD_HPG = SSD_HEADS // SSD_GROUPS
SSD_STATE = 128
SSD_CONV_WIDTH = 5
SSD_CONV_PAD = (SSD_CONV_WIDTH - 1) // 2
SSD_CHUNK = 128
SSD_XBC = SSD_WIDTH + 2 * SSD_GROUPS * SSD_STATE
ATT_HEADS = 6
ATT_HEAD_DIM = 64
ATT_WIDTH = ATT_HEADS * 2 * ATT_HEAD_DIM
REL_BUCKETS = 32
REL_MAX_DIST = 128
D_FF = 4 * D_MODEL
LOG2E = math.log2(math.e)

LANES = 128
SUBLANES = 8
VMEM_LIMIT = 56 * 1024 * 1024

SSD_DT_COLS = LANES
SSD_SLAB = SSD_WIDTH + SSD_XBC + SSD_DT_COLS
CONV_SLAB = 2 * CONV_CH
ATT_SLAB = 3 * ATT_WIDTH
SSD_GW = SSD_HPG * SSD_HEAD_DIM

ATT_TILE = 512
ATT_QTILES = 2
N_BIAS_TILES = 5
ATT_ROWS = 32
CONV_TILE = 128
CONV_HALO = 16
CONV_ROWS = 32
SSD_HALO = SUBLANES
SSD_BATCH = 4
ROW_TILE = 512
DENSE_TILE = 1024
FF_TILE = 512
CAST_BLOCK_BYTES = 8 * 1024 * 1024


def _cparams(sem):
    return pltpu.CompilerParams(dimension_semantics=sem, vmem_limit_bytes=VMEM_LIMIT)


def _rms(x, g):
    ms = jnp.mean(x * x, axis=-1, keepdims=True)
    return (x * lax.rsqrt(ms + EPS)) * g


def _silu(x):
    return x * jax.nn.sigmoid(x)


def _dot(a, b):
    return jnp.dot(a, b, preferred_element_type=F32)


def _split3(x):
    hi = x.astype(BF16)
    r1 = x - hi.astype(F32)
    mid = r1.astype(BF16)
    lo = (r1 - mid.astype(F32)).astype(BF16)
    return hi, mid, lo


def _select_dot(sel, x):
    sel = sel.astype(BF16)
    hi, mid, lo = _split3(x)
    return _dot(sel, hi) + (_dot(sel, mid) + _dot(sel, lo))


def _dot_select(x, sel):
    sel = sel.astype(BF16)
    hi = x.astype(BF16)
    mid = (x - hi.astype(F32)).astype(BF16)
    return _dot(hi, sel) + _dot(mid, sel)


def _dot_nt(a, b):
    return lax.dot_general(a, b, (((1,), (1,)), ((), ())), preferred_element_type=F32)


def _conv_tile(buf_ref, shift_ref, w, b_ref, g_ref, beta_ref, o_ref, row0):
    span = CONV_TILE + 2 * CONV_HALO - SUBLANES
    for b in range(1, SUBLANES):
        shift_ref[b - 1] = buf_ref[b:b + span, :]
    for r0 in range(0, CONV_TILE, CONV_ROWS):
        acc = jnp.broadcast_to(b_ref[...], (CONV_ROWS, CONV_CH))
        for k in range(CONV_WIDTH):
            start = r0 + CONV_HALO - CONV_PAD + k
            b, base = start % SUBLANES, start - start % SUBLANES
            window = (buf_ref[base:base + CONV_ROWS, :] if b == 0
                      else shift_ref[b - 1, base:base + CONV_ROWS, :])
            acc = acc + w[k:k + 1, :] * window
        mu = jnp.mean(acc, axis=-1, keepdims=True)
        xc = acc - mu
        var = jnp.mean(xc * xc, axis=-1, keepdims=True)
        y = xc * lax.rsqrt(var + EPS) * g_ref[...] + beta_ref[...]
        o_ref[row0 + r0:row0 + r0 + CONV_ROWS, :] = _silu(y).astype(o_ref.dtype)


def _proj_conv_kernel(x_ref, g_ref, wc_ref, ws_ref, wa_ref, cw_ref, cb_ref, cg_ref, cbeta_ref,
                      os_ref, oa_ref, oy_ref, ubuf_ref, unext_ref, buf_ref, shift_ref,
                      *, tiles_per_seq):
    tm, halo = ROW_TILE, CONV_HALO
    i = pl.program_id(0)

    @pl.when(i == 0)
    def _():
        ubuf_ref[...] = jnp.zeros_like(ubuf_ref)
        unext_ref[...] = jnp.zeros_like(unext_ref)

    ubuf_ref[0:halo, :] = ubuf_ref[tm:tm + halo, :]
    ubuf_ref[halo:halo + tm, :] = unext_ref[...]

    xn = _rms(x_ref[...], g_ref[...]).astype(BF16)
    pc = _dot(xn, wc_ref[...])
    u = pc[:, :CONV_CH] * jax.nn.sigmoid(pc[:, CONV_CH:])
    unext_ref[...] = u
    ubuf_ref[halo + tm:, :] = u[0:halo, :]
    os_ref[...] = _dot(xn, ws_ref[...])
    oa_ref[...] = _dot(xn, wa_ref[...]).astype(oa_ref.dtype)

    pos = lax.rem(i + tiles_per_seq - 1, tiles_per_seq)
    w = cw_ref[...]
    n_sub = tm // CONV_TILE
    for st in range(n_sub):
        base = st * CONV_TILE
        buf_ref[...] = ubuf_ref[base:base + CONV_TILE + 2 * halo, :]
        if st == 0:
            buf_ref[0:halo, :] = jnp.where(pos > 0, ubuf_ref[0:halo, :], 0.0)
        if st == n_sub - 1:
            buf_ref[halo + CONV_TILE:, :] = jnp.where(pos < tiles_per_seq - 1,
                                                      ubuf_ref[halo + tm:, :], 0.0)
        _conv_tile(buf_ref, shift_ref, w, cb_ref, cg_ref, cbeta_ref, oy_ref, base)


def proj_conv(x, g, w_conv, w_ssd, w_att, cw, cb, cg, cbeta, seq_len, tm=ROW_TILE):
    m, d = x.shape
    assert seq_len % tm == 0 and m % seq_len == 0
    n = m // tm
    resident = lambda w: pl.BlockSpec(w.shape, lambda i: (0, 0), pipeline_mode=pl.Buffered(1))
    cur = lambda cols: pl.BlockSpec((tm, cols), lambda i: (jnp.minimum(i, n - 1), 0))
    vec = lambda: pl.BlockSpec((1, CONV_CH), lambda i: (0, 0))
    return pl.pallas_call(
        functools.partial(_proj_conv_kernel, tiles_per_seq=seq_len // tm),
        grid=(n + 1,),
        in_specs=[cur(d), pl.BlockSpec((1, d), lambda i: (0, 0)),
                  resident(w_conv), resident(w_ssd), resident(w_att),
                  pl.BlockSpec((CONV_WIDTH, CONV_CH), lambda i: (0, 0)), vec(), vec(), vec()],
        out_specs=[cur(SSD_SLAB), cur(ATT_SLAB),
                   pl.BlockSpec((tm, CONV_CH), lambda i: (jnp.maximum(i - 1, 0), 0))],
        out_shape=[jax.ShapeDtypeStruct((m, SSD_SLAB), F32),
                   jax.ShapeDtypeStruct((m, ATT_SLAB), BF16),
                   jax.ShapeDtypeStruct((m, CONV_CH), BF16)],
        scratch_shapes=[pltpu.VMEM((tm + 2 * CONV_HALO, CONV_CH), F32),
                        pltpu.VMEM((tm, CONV_CH), F32),
                        pltpu.VMEM((CONV_TILE + 2 * CONV_HALO, CONV_CH), F32),
                        pltpu.VMEM((SUBLANES - 1, CONV_TILE + 2 * CONV_HALO - SUBLANES, CONV_CH),
                                   F32)],
        compiler_params=_cparams(("arbitrary",)),
        name="proj_conv",
    )(x, g, w_conv, w_ssd, w_att, cw, cb, cg, cbeta)


def _ssd_chunk(xbc, dt_raw, dtb, alog, expand, state_ref, bi, *, reverse):
    q = SSD_CHUNK
    xs = xbc[:, :SSD_WIDTH]
    bmat = xbc[:, SSD_WIDTH:SSD_WIDTH + SSD_GROUPS * SSD_STATE]
    cmat = xbc[:, SSD_WIDTH + SSD_GROUPS * SSD_STATE:]

    dt = jax.nn.softplus(dt_raw + dtb)
    dta = dt * (-jnp.exp(alog))
    row = lax.broadcasted_iota(jnp.int32, (q, q), 0)
    col = lax.broadcasted_iota(jnp.int32, (q, q), 1)
    keep = (row <= col) if reverse else (row >= col)
    a_cs = _select_dot(keep, dta)
    a_tot = jnp.sum(dta, axis=0, keepdims=True)
    ea = jnp.exp(a_cs)
    dec = jnp.exp(a_tot - a_cs)
    cdec = jnp.broadcast_to(jnp.exp(a_tot), (SUBLANES, LANES))
    ex = _dot_select(jnp.concatenate([dt, ea, dec, cdec], axis=0), expand)
    dtx, eax, decx, cdecx = ex[0:q], ex[q:2 * q], ex[2 * q:3 * q], ex[3 * q:3 * q + 1]
    xdt = xs * dtx
    xdt_b = xdt.astype(BF16)
    xd_b = (xdt * decx).astype(BF16)
    a_cs_t = a_cs.T
    lane = lax.broadcasted_iota(jnp.int32, (q, LANES), 1)
    lo_half = lane < SSD_HEAD_DIM
    lane0 = SSD_HEADS if reverse else 0

    y_groups = []
    for g in range(SSD_GROUPS):
        gs = slice(g * SSD_GW, (g + 1) * SSD_GW)
        bg = bmat[:, g * SSD_STATE:(g + 1) * SSD_STATE]
        cg_b = cmat[:, g * SSD_STATE:(g + 1) * SSD_STATE].astype(BF16)
        cb = _dot_nt(cg_b, bg.astype(BF16))
        prev = state_ref[bi, :, gs]
        y_off = _dot(cg_b, prev.astype(BF16)) * eax[:, gs]
        new_states = _dot(bg.T.astype(BF16), xd_b[:, gs])
        state_ref[bi, :, gs] = prev * cdecx[:, gs] + new_states
        pairs = []
        for j in range(SSD_HPG // 2):
            xpair = xdt_b[:, g * SSD_GW + j * LANES:g * SSD_GW + (j + 1) * LANES]
            ypair = None
            for half in range(2):
                ln = lane0 + g * SSD_HPG + 2 * j + half
                seg = a_cs[:, ln:ln + 1] - a_cs_t[ln:ln + 1, :]
                decay = jnp.exp(jnp.where(keep, seg, -jnp.inf))
                mixer = (cb * decay).astype(BF16)
                xh = jnp.where(lo_half if half == 0 else jnp.logical_not(lo_half), xpair,
                               jnp.zeros_like(xpair))
                part = _dot(mixer, xh)
                ypair = part if ypair is None else ypair + part
            pairs.append(ypair)
        y_groups.append(jnp.concatenate(pairs, axis=1) + y_off)
    return jnp.concatenate(y_groups, axis=1), xs


def _ssd_fwd_kernel(cur_ref, left_ref, right_ref, cw_ref, cb_ref, dtb_ref, alog_ref, exp_ref,
                    y_ref, xc_ref, state_ref, buf_ref):
    q = SSD_CHUNK
    c = pl.program_id(1)
    nc = pl.num_programs(1)

    @pl.when(c == 0)
    def _():
        state_ref[...] = jnp.zeros_like(state_ref)

    xbc_lo, xbc_hi = SSD_WIDTH, SSD_WIDTH + SSD_XBC
    cw = cw_ref[...]
    for bi in range(SSD_BATCH):
        blk = cur_ref[bi]
        buf_ref[bi, 0:SSD_HALO, :] = jnp.where(c > 0, left_ref[bi][:, xbc_lo:xbc_hi], 0.0)
        buf_ref[bi, SSD_HALO:SSD_HALO + q, :] = blk[:, xbc_lo:xbc_hi]
        buf_ref[bi, SSD_HALO + q:, :] = jnp.where(c < nc - 1, right_ref[bi][:, xbc_lo:xbc_hi], 0.0)
        acc = jnp.broadcast_to(cb_ref[...], (q, SSD_XBC))
        for k in range(SSD_CONV_WIDTH):
            start = SSD_HALO - SSD_CONV_PAD + k
            acc = acc + cw[k:k + 1, :] * buf_ref[bi, start:start + q, :]
        xbc = _silu(acc)
        xc_ref[bi] = xbc
        y, _ = _ssd_chunk(xbc, blk[:, xbc_hi:], dtb_ref[...], alog_ref[...], exp_ref[...],
                          state_ref, bi, reverse=False)
        y_ref[bi] = y


def _ssd_bwd_kernel(xc_ref, dt_ref, z_ref, yf_ref, dtb_ref, alog_ref, exp_ref, dskip_ref, ng_ref,
                    o_ref, state_ref):
    c = pl.program_id(1)

    @pl.when(c == 0)
    def _():
        state_ref[...] = jnp.zeros_like(state_ref)

    for bi in range(SSD_BATCH):
        y, xs = _ssd_chunk(xc_ref[bi], dt_ref[bi], dtb_ref[...], alog_ref[...], exp_ref[...],
                           state_ref, bi, reverse=True)
        y = y + yf_ref[bi] + dskip_ref[...] * xs
        o_ref[bi] = _rms(y * _silu(z_ref[bi]), ng_ref[...]).astype(o_ref.dtype)


def ssd_mixer(ps, cw, cb, dtb, alog, expand_f, expand_b, dskip, ng):
    bsz, s, _ = ps.shape
    q, nb = SSD_CHUNK, SSD_BATCH
    assert bsz % nb == 0 and s % q == 0
    nc = s // q
    hb = q // SSD_HALO
    last_halo = s // SSD_HALO - 1
    full = lambda shape: pl.BlockSpec(shape, lambda i, c: (0,) * len(shape))
    params = [full((1, LANES)), full((1, LANES)), full((LANES, SSD_WIDTH))]
    state = pltpu.VMEM((nb, SSD_STATE, SSD_WIDTH), F32)
    fwd = lambda width, col=0: pl.BlockSpec((nb, q, width), lambda i, c: (i, c, col))
    y_f, xc = pl.pallas_call(
        _ssd_fwd_kernel,
        grid=(bsz // nb, nc),
        in_specs=[fwd(SSD_SLAB),
                  pl.BlockSpec((nb, SSD_HALO, SSD_SLAB),
                               lambda i, c: (i, jnp.maximum(c * hb - 1, 0), 0)),
                  pl.BlockSpec((nb, SSD_HALO, SSD_SLAB),
                               lambda i, c: (i, jnp.minimum((c + 1) * hb, last_halo), 0)),
                  full((SSD_CONV_WIDTH, SSD_XBC)), full((1, SSD_XBC))] + params,
        out_specs=[fwd(SSD_WIDTH), fwd(SSD_XBC)],
        out_shape=[jax.ShapeDtypeStruct((bsz, s, SSD_WIDTH), F32),
                   jax.ShapeDtypeStruct((bsz, s, SSD_XBC), F32)],
        scratch_shapes=[state, pltpu.VMEM((nb, q + 2 * SSD_HALO, SSD_XBC), F32)],
        compiler_params=_cparams(("parallel", "arbitrary")),
        name="ssd_fwd",
    )(ps, ps, ps, cw, cb, dtb, alog, expand_f)
    bwd = lambda width, col=0: pl.BlockSpec((nb, q, width), lambda i, c: (i, nc - 1 - c, col))
    dt_col = (SSD_WIDTH + SSD_XBC) // SSD_DT_COLS
    return pl.pallas_call(
        _ssd_bwd_kernel,
        grid=(bsz // nb, nc),
        in_specs=[bwd(SSD_XBC), bwd(SSD_DT_COLS, dt_col), bwd(SSD_WIDTH), bwd(SSD_WIDTH)]
                 + params + [full((1, SSD_WIDTH)), full((1, SSD_WIDTH))],
        out_specs=bwd(SSD_WIDTH),
        out_shape=jax.ShapeDtypeStruct((bsz, s, SSD_WIDTH), BF16),
        scratch_shapes=[state],
        compiler_params=_cparams(("parallel", "arbitrary")),
        name="ssd_bwd",
    )(xc, ps, ps, y_f, dtb, alog, expand_b, dskip, ng)


def _bias_tiles_kernel(bucket_ref, table_ref, o_ref):
    h = pl.program_id(0)
    bucket = bucket_ref[0]
    out = jnp.zeros(bucket.shape, F32)
    for b in range(REL_BUCKETS):
        out = jnp.where(bucket == b, table_ref[b, h] * LOG2E, out)
    o_ref[0, 0] = out


def bias_tiles(buckets, rel_bias):
    nt, t, _ = buckets.shape
    return pl.pallas_call(
        _bias_tiles_kernel,
        grid=(ATT_HEADS, nt),
        in_specs=[pl.BlockSpec((1, t, t), lambda h, d: (d, 0, 0)),
                  pl.BlockSpec(memory_space=pltpu.SMEM)],
        out_specs=pl.BlockSpec((1, 1, t, t), lambda h, d: (h, d, 0, 0)),
        out_shape=jax.ShapeDtypeStruct((ATT_HEADS, nt, t, t), F32),
        compiler_params=_cparams(("parallel", "parallel")),
        name="bias_tiles",
    )(buckets, rel_bias)


def _rel_bucket(rel):
    nb = REL_BUCKETS // 2
    max_exact = nb // 2
    ret = jnp.where(rel > 0, nb, 0)
    n = jnp.abs(rel)
    nf = jnp.maximum(n, 1).astype(jnp.float32)
    large = max_exact + (jnp.log(nf / max_exact) / math.log(REL_MAX_DIST / max_exact)
                         * (nb - max_exact)).astype(jnp.int32)
    large = jnp.minimum(large, nb - 1)
    return ret + jnp.where(n < max_exact, n, large)


def _bucket_tiles(t):
    i = jnp.arange(t, dtype=jnp.int32)
    half = N_BIAS_TILES // 2
    offs = (jnp.arange(N_BIAS_TILES, dtype=jnp.int32) - half) * t
    rel = offs[:, None, None] + i[None, None, :] - i[None, :, None]
    return _rel_bucket(rel).astype(jnp.int32)


def _attn_kernel(q_ref, k_ref, v_ref, bias_ref, lam_ref, g_ref, o_ref,
                 qz_ref, s_ref, p_ref, m_ref, l_ref, alpha_ref, acc_ref, *, lam_init, nk):
    t = ATT_TILE
    tq = ATT_QTILES * t
    qi = pl.program_id(2)
    qb = q_ref[0]
    lane = lax.broadcasted_iota(jnp.int32, (tq, 2 * ATT_HEAD_DIM), 1)
    qz_ref[0] = jnp.where(lane < ATT_HEAD_DIM, qb, jnp.zeros_like(qb))
    qz_ref[1] = jnp.where(lane >= ATT_HEAD_DIM, qb, jnp.zeros_like(qb))
    m_ref[...] = jnp.full(m_ref.shape, -jnp.inf, F32)
    l_ref[...] = jnp.zeros_like(l_ref)
    acc_ref[...] = jnp.zeros_like(acc_ref)
    half = N_BIAS_TILES // 2
    rb = ATT_ROWS

    def key_rows(kj):
        start = kj * t
        return pl.ds(start if isinstance(start, int) else pl.multiple_of(start, t), t)

    def apply_pv(slot, kj):
        vb = v_ref[0, key_rows(kj), :]
        for mp in range(2):
            acc_ref[mp] = alpha_ref[slot, mp] * acc_ref[mp] + _dot(p_ref[slot, mp], vb)

    def logits(slot, kj):
        kb = k_ref[0, key_rows(kj), :]
        for mp in range(2):
            s_ref[slot, mp] = _dot_nt(qz_ref[mp], kb)

    def softmax(slot, kj):
        tiles = [jnp.clip(kj - (ATT_QTILES * qi + u), -half, half) + half
                 for u in range(ATT_QTILES)]
        for mp in range(2):
            for r0 in range(0, tq, rb):
                rows = slice(r0, r0 + rb)
                bias_rows = slice(r0 % t, r0 % t + rb)
                s = s_ref[slot, mp, rows, :] + bias_ref[0, tiles[r0 // t], bias_rows, :]
                m_old = m_ref[mp, rows, :]
                m_new = jnp.maximum(m_old, jnp.max(s, axis=-1, keepdims=True))
                alpha = jnp.exp2(m_old - m_new)
                p = jnp.exp2(s - jnp.concatenate([m_new] * (t // LANES), axis=1))
                l_ref[mp, rows, :] = (alpha * l_ref[mp, rows, :]
                                      + jnp.sum(p, axis=-1, keepdims=True))
                m_ref[mp, rows, :] = m_new
                alpha_ref[slot, mp, rows, :] = alpha
                p_ref[slot, mp, rows, :] = p.astype(BF16)

    def pair(kk, carry, first=False):
        for slot in range(2):
            kj = 2 * kk + slot
            logits(slot, kj)
            if not (first and slot == 0):
                apply_pv(1 - slot, kj - 1)
            softmax(slot, kj)
        return carry

    pair(0, 0, first=True)
    lax.fori_loop(1, nk // 2, pair, 0)
    apply_pv(1, nk - 1)

    lp = lam_ref[...]
    lam = (jnp.exp(jnp.sum(lp[0:1] * lp[1:2], axis=-1, keepdims=True))
           - jnp.exp(jnp.sum(lp[2:3] * lp[3:4], axis=-1, keepdims=True)) + lam_init)
    o = acc_ref[0] / l_ref[0] - lam * (acc_ref[1] / l_ref[1])
    o_ref[0] = (_rms(o, g_ref[...]) * (1.0 - lam_init)).astype(o_ref.dtype)


def diff_attention(qkv, bias, lam_params, subln_g, lam_init):
    bsz, s, _ = qkv.shape
    t = ATT_TILE
    tq = ATT_QTILES * t
    nk = s // t
    hw = 2 * ATT_HEAD_DIM
    assert nk % 2 == 0 and s % tq == 0
    stat = pltpu.VMEM((2, tq, LANES), F32)
    return pl.pallas_call(
        functools.partial(_attn_kernel, lam_init=lam_init, nk=nk),
        grid=(bsz, ATT_HEADS, s // tq),
        in_specs=[
            pl.BlockSpec((1, tq, hw), lambda b, h, i: (b, i, h)),
            pl.BlockSpec((1, s, hw), lambda b, h, i: (b, 0, ATT_HEADS + h)),
            pl.BlockSpec((1, s, hw), lambda b, h, i: (b, 0, 2 * ATT_HEADS + h)),
            pl.BlockSpec((1, N_BIAS_TILES, t, t), lambda b, h, i: (h, 0, 0, 0)),
            pl.BlockSpec((4, ATT_HEAD_DIM), lambda b, h, i: (0, 0)),
            pl.BlockSpec((1, hw), lambda b, h, i: (0, 0)),
        ],
        out_specs=pl.BlockSpec((1, tq, hw), lambda b, h, i: (b, i, h)),
        out_shape=jax.ShapeDtypeStruct((bsz, s, ATT_WIDTH), BF16),
        scratch_shapes=[pltpu.VMEM((2, tq, hw), BF16),
                        pltpu.VMEM((2, 2, tq, t), F32),
                        pltpu.VMEM((2, 2, tq, t), BF16),
                        stat, stat,
                        pltpu.VMEM((2, 2, tq, LANES), F32),
                        pltpu.VMEM((2, tq, hw), F32)],
        compiler_params=_cparams(("parallel", "parallel", "parallel")),
        name="diff_attention",
    )(qkv, qkv, qkv, bias, lam_params, subln_g)


def _out_proj_kernel(yc_ref, ys_ref, ya_ref, h_ref, w_ref, o_ref):
    c0, c1 = CONV_CH, CONV_CH + SSD_WIDTH
    acc = _dot(yc_ref[...], w_ref[0:c0, :])
    acc = acc + _dot(ys_ref[...], w_ref[c0:c1, :])
    acc = acc + _dot(ya_ref[...], w_ref[c1:, :])
    o_ref[...] = h_ref[...] + acc


def out_proj(yc, ys, ya, h, w, tm=DENSE_TILE):
    m, d = h.shape
    row = lambda n: pl.BlockSpec((tm, n), lambda i: (i, 0))
    return pl.pallas_call(
        _out_proj_kernel,
        grid=(m // tm,),
        in_specs=[row(CONV_CH), row(SSD_WIDTH), row(ATT_WIDTH), row(d),
                  pl.BlockSpec(w.shape, lambda i: (0, 0), pipeline_mode=pl.Buffered(1))],
        out_specs=row(d),
        out_shape=jax.ShapeDtypeStruct((m, d), F32),
        compiler_params=_cparams(("parallel",)),
        name="out_proj",
    )(yc, ys, ya, h, w)


def _mlp_kernel(h_ref, g_ref, wu_ref, wd_ref, o_ref, xn_ref):
    f = pl.program_id(1)

    @pl.when(f == 0)
    def _():
        h = h_ref[...]
        xn_ref[...] = _rms(h, g_ref[...]).astype(BF16)
        o_ref[...] = h

    hid = jnp.square(jnp.maximum(_dot(xn_ref[...], wu_ref[...]), 0.0)).astype(BF16)
    o_ref[...] += _dot(hid, wd_ref[...])


def mlp(h, g, wu, wd, tm=DENSE_TILE, tf=FF_TILE):
    m, d = h.shape
    ff = wu.shape[1]
    return pl.pallas_call(
        _mlp_kernel,
        grid=(m // tm, ff // tf),
        in_specs=[pl.BlockSpec((tm, d), lambda i, f: (i, 0)),
                  pl.BlockSpec((1, d), lambda i, f: (0, 0)),
                  pl.BlockSpec((d, tf), lambda i, f: (0, f)),
                  pl.BlockSpec((tf, d), lambda i, f: (f, 0))],
        out_specs=pl.BlockSpec((tm, d), lambda i, f: (i, 0)),
        out_shape=jax.ShapeDtypeStruct((m, d), F32),
        scratch_shapes=[pltpu.VMEM((tm, d), BF16)],
        compiler_params=_cparams(("parallel", "arbitrary")),
        name="mlp",
    )(h, g, wu, wd)


def _ple_kernel(h_ref, p_ref, g_ref, wg_ref, wp_ref, gf_ref, o_ref, *, final):
    h = h_ref[...]
    gate = jax.nn.sigmoid(_dot(_rms(h, g_ref[...]).astype(BF16), wg_ref[...]))
    out = h + _dot(p_ref[...].astype(BF16), wp_ref[...]) * gate
    if final:
        out = _rms(out, gf_ref[...])
    o_ref[...] = out


def ple(h, p, g, wg, wp, gf, final, tm=DENSE_TILE):
    m, d = h.shape
    vec = pl.BlockSpec((1, d), lambda i: (0, 0))
    return pl.pallas_call(
        functools.partial(_ple_kernel, final=final),
        grid=(m // tm,),
        in_specs=[pl.BlockSpec((tm, d), lambda i: (i, 0)),
                  pl.BlockSpec((tm, PLE_DIM), lambda i: (i, 0)),
                  vec,
                  pl.BlockSpec((d, d), lambda i: (0, 0), pipeline_mode=pl.Buffered(1)),
                  pl.BlockSpec((PLE_DIM, d), lambda i: (0, 0), pipeline_mode=pl.Buffered(1)),
                  vec],
        out_specs=pl.BlockSpec((tm, d), lambda i: (i, 0)),
        out_shape=jax.ShapeDtypeStruct((m, d), F32),
        compiler_params=_cparams(("parallel",)),
        name="ple_final" if final else "ple",
    )(h, p, g, wg, wp, gf)


def _cast_kernel(w_ref, o_ref):
    o_ref[...] = w_ref[0].astype(o_ref.dtype)


def cast_layer(w, layer):
    _, r, c = w.shape
    tr = min(r, CAST_BLOCK_BYTES // (4 * c))
    assert r % tr == 0 and tr % SUBLANES == 0
    return pl.pallas_call(
        _cast_kernel,
        grid=(r // tr,),
        in_specs=[pl.BlockSpec((1, tr, c), lambda j: (layer, j, 0))],
        out_specs=pl.BlockSpec((tr, c), lambda j: (j, 0)),
        out_shape=jax.ShapeDtypeStruct((r, c), BF16),
        compiler_params=_cparams(("parallel",)),
        name="cast_layer",
    )(w)


def _expand_matrix(direction):
    e = np.zeros((LANES, SSD_WIDTH), np.float32)
    for hd in range(SSD_HEADS):
        e[direction * SSD_HEADS + hd, hd * SSD_HEAD_DIM:(hd + 1) * SSD_HEAD_DIM] = 1.0
    return jnp.asarray(e)


def _pad_lanes(v):
    return jnp.pad(v.reshape(1, -1), ((0, 0), (0, LANES - v.size)))


def _layer_params(i, norm_mix_g, w_in, conv_w, conv_b, conv_norm_g, conv_norm_b, ssd_conv_w,
                  ssd_conv_b, ssd_dt_bias, ssd_a_log, ssd_d, ssd_norm_g, lambda_q1, lambda_k1,
                  lambda_q2, lambda_k2, attn_subln_g, w_out, norm_mlp_g, w_up, w_down,
                  norm_ple_g, w_ple, w_ple_gate):
    row = lambda v: v.reshape(1, -1)
    w = w_in[i]
    c_conv = 2 * CONV_CH
    c_dt = c_conv + SSD_WIDTH + SSD_XBC
    c_att = c_dt + 2 * SSD_HEADS
    w_ssd = jnp.pad(w[:, c_conv:c_att], ((0, 0), (0, SSD_DT_COLS - 2 * SSD_HEADS)))
    return dict(
        norm_mix_g=row(norm_mix_g[i]),
        w_conv=w[:, :c_conv].astype(BF16),
        w_ssd=w_ssd.astype(BF16),
        w_att=jnp.concatenate([w[:, c_att:c_att + ATT_WIDTH] * (LOG2E * ATT_HEAD_DIM ** -0.5),
                               w[:, c_att + ATT_WIDTH:]], axis=1).astype(BF16),
        conv_w=conv_w[i], conv_b=row(conv_b[i]), conv_g=row(conv_norm_g[i]),
        conv_beta=row(conv_norm_b[i]),
        ssd_cw=ssd_conv_w[i], ssd_cb=row(ssd_conv_b[i]),
        ssd_dtb=_pad_lanes(ssd_dt_bias[i]), ssd_alog=_pad_lanes(ssd_a_log[i]),
        ssd_dskip=row(jnp.repeat(ssd_d[i], SSD_HEAD_DIM)), ssd_ng=row(ssd_norm_g[i]),
        lam=jnp.stack([lambda_q1[i], lambda_k1[i], lambda_q2[i], lambda_k2[i]]),
        subln_g=row(attn_subln_g[i]),
        w_out=cast_layer(w_out, i),
        norm_mlp_g=row(norm_mlp_g[i]),
        w_up=cast_layer(w_up, i), w_down=cast_layer(w_down, i),
        norm_ple_g=row(norm_ple_g[i]),
        w_ple=cast_layer(w_ple, i), w_gate=cast_layer(w_ple_gate, i),
    )


def _run(x, p, layers, bias, expand_f, expand_b, final_g):
    bsz, s, d = x.shape
    m = bsz * s
    h = x.reshape(m, d)
    for i, lp in enumerate(layers):
        lam_init = 0.8 - 0.6 * math.exp(-0.3 * i)
        ps, pa, y_conv = proj_conv(h, lp["norm_mix_g"], lp["w_conv"], lp["w_ssd"], lp["w_att"],
                                   lp["conv_w"], lp["conv_b"], lp["conv_g"], lp["conv_beta"], s)
        ps = ps.reshape(bsz, s, SSD_SLAB)
        pa = pa.reshape(bsz, s, ATT_SLAB)
        y_ssd = ssd_mixer(ps, lp["ssd_cw"], lp["ssd_cb"], lp["ssd_dtb"], lp["ssd_alog"],
                          expand_f, expand_b, lp["ssd_dskip"], lp["ssd_ng"])
        y_att = diff_attention(pa, bias, lp["lam"], lp["subln_g"], lam_init)
        h = out_proj(y_conv, y_ssd.reshape(m, SSD_WIDTH),
                     y_att.reshape(m, ATT_WIDTH), h, lp["w_out"])
        h = mlp(h, lp["norm_mlp_g"], lp["w_up"], lp["w_down"])
        h = ple(h, p[i].reshape(m, PLE_DIM), lp["norm_ple_g"], lp["w_gate"], lp["w_ple"],
                final_g, final=(i == len(layers) - 1))
    return h.reshape(bsz, s, d)


def kernel(x_prompt, x_sample, p_prompt, p_sample, norm_mix_g, w_in, conv_w, conv_b, conv_norm_g,
           conv_norm_b, ssd_conv_w, ssd_conv_b, ssd_dt_bias, ssd_a_log, ssd_d, ssd_norm_g,
           lambda_q1, lambda_k1, lambda_q2, lambda_k2, attn_subln_g, rel_bias, w_out, norm_mlp_g,
           w_up, w_down, norm_ple_g, w_ple, w_ple_gate, final_norm_g):
    layers = [_layer_params(i, norm_mix_g, w_in, conv_w, conv_b, conv_norm_g, conv_norm_b,
                            ssd_conv_w, ssd_conv_b, ssd_dt_bias, ssd_a_log, ssd_d, ssd_norm_g,
                            lambda_q1, lambda_k1, lambda_q2, lambda_k2, attn_subln_g, w_out,
                            norm_mlp_g, w_up, w_down, norm_ple_g, w_ple, w_ple_gate)
              for i in range(DEPTH)]
    bias = bias_tiles(_bucket_tiles(ATT_TILE), rel_bias)
    expand_f, expand_b = _expand_matrix(0), _expand_matrix(1)
    final_g = final_norm_g.reshape(1, -1)
    y_prompt = _run(x_prompt, p_prompt, layers, bias, expand_f, expand_b, final_g)
    y_sample = _run(x_sample, p_sample, layers, bias, expand_f, expand_b, final_g)
    return (y_prompt, y_sample)
```

```python
import functools
import math

import jax
import jax.numpy as jnp
import numpy as np
from jax import lax
from jax.experimental import pallas as pl
from jax.experimental.pallas import tpu as pltpu

F32 = jnp.float32
BF16 = jnp.bfloat16

D_MODEL = 2048
DEPTH = 2
PLE_DIM = 256
EPS = 1e-6
CONV_CH = 512
CONV_WIDTH = 31
CONV_PAD = (CONV_WIDTH - 1) // 2
SSD_HEAD_DIM = 64
SSD_HEADS = 12
SSD_WIDTH = SSD_HEADS * SSD_HEAD_DIM
SSD_GROUPS = 2
SSD_HPG = SSD_HEADS // SSD_GROUPS
SSD_STATE = 128
SSD_CONV_WIDTH = 5
SSD_CONV_PAD = (SSD_CONV_WIDTH - 1) // 2
SSD_CHUNK = 128
SSD_XBC = SSD_WIDTH + 2 * SSD_GROUPS * SSD_STATE
ATT_HEADS = 6
ATT_HEAD_DIM = 64
ATT_WIDTH = ATT_HEADS * 2 * ATT_HEAD_DIM
REL_BUCKETS = 32
REL_MAX_DIST = 128
D_FF = 4 * D_MODEL
LOG2E = math.log2(math.e)

LANES = 128
SUBLANES = 8
VMEM_LIMIT = 56 * 1024 * 1024

SSD_DT_COLS = LANES
SSD_SLAB = SSD_WIDTH + SSD_XBC + SSD_DT_COLS
CONV_SLAB = 2 * CONV_CH
ATT_SLAB = 3 * ATT_WIDTH
SSD_GW = SSD_HPG * SSD_HEAD_DIM

ATT_TILE = 512
ATT_QTILES = 2
N_BIAS_TILES = 5
ATT_ROWS = 32
CONV_TILE = 128
CONV_HALO = 16
CONV_ROWS = 32
SSD_HALO = SUBLANES
SSD_BATCH = 4
ROW_TILE = 512
DENSE_TILE = 1024
FF_TILE = 512
CAST_BLOCK_BYTES = 8 * 1024 * 1024
SPLIT_ROWS = 256


def _cparams(sem):
    return pltpu.CompilerParams(dimension_semantics=sem, vmem_limit_bytes=VMEM_LIMIT)


def _rms(x, g):
    ms = jnp.mean(x * x, axis=-1, keepdims=True)
    return (x * lax.rsqrt(ms + EPS)) * g


def _silu(x):
    return x * jax.nn.sigmoid(x)


def _dot(a, b):
    return jnp.dot(a, b, preferred_element_type=F32)


def _split3(x):
    hi = x.astype(BF16)
    r1 = x - hi.astype(F32)
    mid = r1.astype(BF16)
    lo = (r1 - mid.astype(F32)).astype(BF16)
    return hi, mid, lo


def _select_dot(sel, x):
    sel = sel.astype(BF16)
    hi, mid, lo = _split3(x)
    return _dot(sel, hi) + (_dot(sel, mid) + _dot(sel, lo))


def _dot_select(x, sel):
    sel = sel.astype(BF16)
    hi = x.astype(BF16)
    mid = (x - hi.astype(F32)).astype(BF16)
    return _dot(hi, sel) + _dot(mid, sel)


def _dot_nt(a, b):
    return lax.dot_general(a, b, (((1,), (1,)), ((), ())), preferred_element_type=F32)


def _conv_tile(buf_ref, shift_ref, w, b_ref, g_ref, beta_ref, o_ref, row0):
    span = CONV_TILE + 2 * CONV_HALO - SUBLANES
    for b in range(1, SUBLANES):
        shift_ref[b - 1] = buf_ref[b:b + span, :]
    for r0 in range(0, CONV_TILE, CONV_ROWS):
        acc = jnp.broadcast_to(b_ref[...], (CONV_ROWS, CONV_CH))
        for k in range(CONV_WIDTH):
            start = r0 + CONV_HALO - CONV_PAD + k
            b, base = start % SUBLANES, start - start % SUBLANES
            window = (buf_ref[base:base + CONV_ROWS, :] if b == 0
                      else shift_ref[b - 1, base:base + CONV_ROWS, :])
            acc = acc + w[k:k + 1, :] * window
        mu = jnp.mean(acc, axis=-1, keepdims=True)
        xc = acc - mu
        var = jnp.mean(xc * xc, axis=-1, keepdims=True)
        y = xc * lax.rsqrt(var + EPS) * g_ref[...] + beta_ref[...]
        o_ref[row0 + r0:row0 + r0 + CONV_ROWS, :] = _silu(y).astype(o_ref.dtype)


def _proj_conv_kernel(x_ref, g_ref, wc_ref, ws_ref, wa_ref, cw_ref, cb_ref, cg_ref, cbeta_ref,
                      os_ref, oa_ref, oy_ref, ubuf_ref, unext_ref, buf_ref, shift_ref,
                      *, tiles_per_seq):
    tm, halo = ROW_TILE, CONV_HALO
    i = pl.program_id(0)

    @pl.when(i == 0)
    def _():
        ubuf_ref[...] = jnp.zeros_like(ubuf_ref)
        unext_ref[...] = jnp.zeros_like(unext_ref)

    ubuf_ref[0:halo, :] = ubuf_ref[tm:tm + halo, :]
    ubuf_ref[halo:halo + tm, :] = unext_ref[...]

    xn = _rms(x_ref[...], g_ref[...]).astype(BF16)
    pc = _dot(xn, wc_ref[...])
    u = pc[:, :CONV_CH] * jax.nn.sigmoid(pc[:, CONV_CH:])
    unext_ref[...] = u
    ubuf_ref[halo + tm:, :] = u[0:halo, :]
    os_ref[...] = _dot(xn, ws_ref[...])
    oa_ref[...] = _dot(xn, wa_ref[...]).astype(oa_ref.dtype)

    pos = lax.rem(i + tiles_per_seq - 1, tiles_per_seq)
    w = cw_ref[...]
    n_sub = tm // CONV_TILE
    for st in range(n_sub):
        base = st * CONV_TILE
        buf_ref[...] = ubuf_ref[base:base + CONV_TILE + 2 * halo, :]
        if st == 0:
            buf_ref[0:halo, :] = jnp.where(pos > 0, ubuf_ref[0:halo, :], 0.0)
        if st == n_sub - 1:
            buf_ref[halo + CONV_TILE:, :] = jnp.where(pos < tiles_per_seq - 1,
                                                      ubuf_ref[halo + tm:, :], 0.0)
        _conv_tile(buf_ref, shift_ref, w, cb_ref, cg_ref, cbeta_ref, oy_ref, base)


def proj_conv(x, g, w_conv, w_ssd, w_att, cw, cb, cg, cbeta, seq_len, tm=ROW_TILE):
    m, d = x.shape
    assert seq_len % tm == 0 and m % seq_len == 0
    n = m // tm
    resident = lambda w: pl.BlockSpec(w.shape, lambda i: (0, 0), pipeline_mode=pl.Buffered(1))
    cur = lambda cols: pl.BlockSpec((tm, cols), lambda i: (jnp.minimum(i, n - 1), 0))
    vec = lambda: pl.BlockSpec((1, CONV_CH), lambda i: (0, 0))
    return pl.pallas_call(
        functools.partial(_proj_conv_kernel, tiles_per_seq=seq_len // tm),
        grid=(n + 1,),
        in_specs=[cur(d), pl.BlockSpec((1, d), lambda i: (0, 0)),
                  resident(w_conv), resident(w_ssd), resident(w_att),
                  pl.BlockSpec((CONV_WIDTH, CONV_CH), lambda i: (0, 0)), vec(), vec(), vec()],
        out_specs=[cur(SSD_SLAB), cur(ATT_SLAB),
                   pl.BlockSpec((tm, CONV_CH), lambda i: (jnp.maximum(i - 1, 0), 0))],
        out_shape=[jax.ShapeDtypeStruct((m, SSD_SLAB), F32),
                   jax.ShapeDtypeStruct((m, ATT_SLAB), BF16),
                   jax.ShapeDtypeStruct((m, CONV_CH), BF16)],
        scratch_shapes=[pltpu.VMEM((tm + 2 * CONV_HALO, CONV_CH), F32),
                        pltpu.VMEM((tm, CONV_CH), F32),
                        pltpu.VMEM((CONV_TILE + 2 * CONV_HALO, CONV_CH), F32),
                        pltpu.VMEM((SUBLANES - 1, CONV_TILE + 2 * CONV_HALO - SUBLANES, CONV_CH),
                                   F32)],
        compiler_params=_cparams(("arbitrary",)),
        name="proj_conv",
    )(x, g, w_conv, w_ssd, w_att, cw, cb, cg, cbeta)


def _ssd_chunk(xbc, dt_raw, dtb, alog, expand, state_ref, bi, *, reverse):
    q = SSD_CHUNK
    xs = xbc[:, :SSD_WIDTH]
    bmat = xbc[:, SSD_WIDTH:SSD_WIDTH + SSD_GROUPS * SSD_STATE]
    cmat = xbc[:, SSD_WIDTH + SSD_GROUPS * SSD_STATE:]

    dt = jax.nn.softplus(dt_raw + dtb)
    dta = dt * (-jnp.exp(alog))
    row = lax.broadcasted_iota(jnp.int32, (q, q), 0)
    col = lax.broadcasted_iota(jnp.int32, (q, q), 1)
    keep = (row <= col) if reverse else (row >= col)
    a_cs = _select_dot(keep, dta)
    a_tot = jnp.sum(dta, axis=0, keepdims=True)
    ea = jnp.exp(a_cs)
    dec = jnp.exp(a_tot - a_cs)
    cdec = jnp.broadcast_to(jnp.exp(a_tot), (SUBLANES, LANES))
    ex = _dot_select(jnp.concatenate([dt, ea, dec, cdec], axis=0), expand)
    dtx, eax, decx, cdecx = ex[0:q], ex[q:2 * q], ex[2 * q:3 * q], ex[3 * q:3 * q + 1]
    xdt = xs * dtx
    xdt_b = xdt.astype(BF16)
    xd_b = (xdt * decx).astype(BF16)
    a_cs_t = a_cs.T
    lane = lax.broadcasted_iota(jnp.int32, (q, LANES), 1)
    lo_half = lane < SSD_HEAD_DIM
    lane0 = SSD_HEADS if reverse else 0

    y_groups = []
    for g in range(SSD_GROUPS):
        gs = slice(g * SSD_GW, (g + 1) * SSD_GW)
        bg = bmat[:, g * SSD_STATE:(g + 1) * SSD_STATE]
        cg_b = cmat[:, g * SSD_STATE:(g + 1) * SSD_STATE].astype(BF16)
        cb = _dot_nt(cg_b, bg.astype(BF16))
        prev = state_ref[bi, :, gs]
        y_off = _dot(cg_b, prev.astype(BF16)) * eax[:, gs]
        new_states = _dot(bg.T.astype(BF16), xd_b[:, gs])
        state_ref[bi, :, gs] = prev * cdecx[:, gs] + new_states
        pairs = []
        for j in range(SSD_HPG // 2):
            xpair = xdt_b[:, g * SSD_GW + j * LANES:g * SSD_GW + (j + 1) * LANES]
            ypair = None
            for half in range(2):
                ln = lane0 + g * SSD_HPG + 2 * j + half
                seg = a_cs[:, ln:ln + 1] - a_cs_t[ln:ln + 1, :]
                decay = jnp.exp(jnp.where(keep, seg, -jnp.inf))
                mixer = (cb * decay).astype(BF16)
                xh = jnp.where(lo_half if half == 0 else jnp.logical_not(lo_half), xpair,
                               jnp.zeros_like(xpair))
                part = _dot(mixer, xh)
                ypair = part if ypair is None else ypair + part
            pairs.append(ypair)
        y_groups.append(jnp.concatenate(pairs, axis=1) + y_off)
    return jnp.concatenate(y_groups, axis=1), xs


def _ssd_fwd_kernel(cur_ref, left_ref, right_ref, cw_ref, cb_ref, dtb_ref, alog_ref, exp_ref,
                    y_ref, xc_ref, state_ref, buf_ref):
    q = SSD_CHUNK
    c = pl.program_id(1)
    nc = pl.num_programs(1)

    @pl.when(c == 0)
    def _():
        state_ref[...] = jnp.zeros_like(state_ref)

    xbc_lo, xbc_hi = SSD_WIDTH, SSD_WIDTH + SSD_XBC
    cw = cw_ref[...]
    for bi in range(SSD_BATCH):
        blk = cur_ref[bi]
        buf_ref[bi, 0:SSD_HALO, :] = jnp.where(c > 0, left_ref[bi][:, xbc_lo:xbc_hi], 0.0)
        buf_ref[bi, SSD_HALO:SSD_HALO + q, :] = blk[:, xbc_lo:xbc_hi]
        buf_ref[bi, SSD_HALO + q:, :] = jnp.where(c < nc - 1, right_ref[bi][:, xbc_lo:xbc_hi], 0.0)
        acc = jnp.broadcast_to(cb_ref[...], (q, SSD_XBC))
        for k in range(SSD_CONV_WIDTH):
            start = SSD_HALO - SSD_CONV_PAD + k
            acc = acc + cw[k:k + 1, :] * buf_ref[bi, start:start + q, :]
        xbc = _silu(acc)
        xc_ref[bi] = xbc
        y, _ = _ssd_chunk(xbc, blk[:, xbc_hi:], dtb_ref[...], alog_ref[...], exp_ref[...],
                          state_ref, bi, reverse=False)
        y_ref[bi] = y


def _ssd_bwd_kernel(xc_ref, dt_ref, z_ref, yf_ref, dtb_ref, alog_ref, exp_ref, dskip_ref, ng_ref,
                    o_ref, state_ref):
    c = pl.program_id(1)

    @pl.when(c == 0)
    def _():
        state_ref[...] = jnp.zeros_like(state_ref)

    for bi in range(SSD_BATCH):
        y, xs = _ssd_chunk(xc_ref[bi], dt_ref[bi], dtb_ref[...], alog_ref[...], exp_ref[...],
                           state_ref, bi, reverse=True)
        y = y + yf_ref[bi] + dskip_ref[...] * xs
        o_ref[bi] = _rms(y * _silu(z_ref[bi]), ng_ref[...]).astype(o_ref.dtype)


def ssd_mixer(ps, cw, cb, dtb, alog, expand_f, expand_b, dskip, ng):
    bsz, s, _ = ps.shape
    q, nb = SSD_CHUNK, SSD_BATCH
    assert bsz % nb == 0 and s % q == 0
    nc = s // q
    hb = q // SSD_HALO
    last_halo = s // SSD_HALO - 1
    full = lambda shape: pl.BlockSpec(shape, lambda i, c: (0,) * len(shape))
    params = [full((1, LANES)), full((1, LANES)), full((LANES, SSD_WIDTH))]
    state = pltpu.VMEM((nb, SSD_STATE, SSD_WIDTH), F32)
    fwd = lambda width, col=0: pl.BlockSpec((nb, q, width), lambda i, c: (i, c, col))
    y_f, xc = pl.pallas_call(
        _ssd_fwd_kernel,
        grid=(bsz // nb, nc),
        in_specs=[fwd(SSD_SLAB),
                  pl.BlockSpec((nb, SSD_HALO, SSD_SLAB),
                               lambda i, c: (i, jnp.maximum(c * hb - 1, 0), 0)),
                  pl.BlockSpec((nb, SSD_HALO, SSD_SLAB),
                               lambda i, c: (i, jnp.minimum((c + 1) * hb, last_halo), 0)),
                  full((SSD_CONV_WIDTH, SSD_XBC)), full((1, SSD_XBC))] + params,
        out_specs=[fwd(SSD_WIDTH), fwd(SSD_XBC)],
        out_shape=[jax.ShapeDtypeStruct((bsz, s, SSD_WIDTH), F32),
                   jax.ShapeDtypeStruct((bsz, s, SSD_XBC), F32)],
        scratch_shapes=[state, pltpu.VMEM((nb, q + 2 * SSD_HALO, SSD_XBC), F32)],
        compiler_params=_cparams(("parallel", "arbitrary")),
        name="ssd_fwd",
    )(ps, ps, ps, cw, cb, dtb, alog, expand_f)
    bwd = lambda width, col=0: pl.BlockSpec((nb, q, width), lambda i, c: (i, nc - 1 - c, col))
    dt_col = (SSD_WIDTH + SSD_XBC) // SSD_DT_COLS
    return pl.pallas_call(
        _ssd_bwd_kernel,
        grid=(bsz // nb, nc),
        in_specs=[bwd(SSD_XBC), bwd(SSD_DT_COLS, dt_col), bwd(SSD_WIDTH), bwd(SSD_WIDTH)]
                 + params + [full((1, SSD_WIDTH)), full((1, SSD_WIDTH))],
        out_specs=bwd(SSD_WIDTH),
        out_shape=jax.ShapeDtypeStruct((bsz, s, SSD_WIDTH), BF16),
        scratch_shapes=[state],
        compiler_params=_cparams(("parallel", "arbitrary")),
        name="ssd_bwd",
    )(xc, ps, ps, y_f, dtb, alog, expand_b, dskip, ng)


def _bias_tiles_kernel(bucket_ref, table_ref, o_ref):
    h = pl.program_id(0)
    bucket = bucket_ref[0]
    out = jnp.zeros(bucket.shape, F32)
    for b in range(REL_BUCKETS):
        out = jnp.where(bucket == b, table_ref[b, h] * LOG2E, out)
    o_ref[0, 0] = out


def bias_tiles(buckets, rel_bias):
    nt, t, _ = buckets.shape
    return pl.pallas_call(
        _bias_tiles_kernel,
        grid=(ATT_HEADS, nt),
        in_specs=[pl.BlockSpec((1, t, t), lambda h, d: (d, 0, 0)),
                  pl.BlockSpec(memory_space=pltpu.SMEM)],
        out_specs=pl.BlockSpec((1, 1, t, t), lambda h, d: (h, d, 0, 0)),
        out_shape=jax.ShapeDtypeStruct((ATT_HEADS, nt, t, t), F32),
        compiler_params=_cparams(("parallel", "parallel")),
        name="bias_tiles",
    )(buckets, rel_bias)


def _rel_bucket(rel):
    nb = REL_BUCKETS // 2
    max_exact = nb // 2
    ret = jnp.where(rel > 0, nb, 0)
    n = jnp.abs(rel)
    nf = jnp.maximum(n, 1).astype(jnp.float32)
    large = max_exact + (jnp.log(nf / max_exact) / math.log(REL_MAX_DIST / max_exact)
                         * (nb - max_exact)).astype(jnp.int32)
    large = jnp.minimum(large, nb - 1)
    return ret + jnp.where(n < max_exact, n, large)


def _bucket_tiles(t):
    i = jnp.arange(t, dtype=jnp.int32)
    half = N_BIAS_TILES // 2
    offs = (jnp.arange(N_BIAS_TILES, dtype=jnp.int32) - half) * t
    rel = offs[:, None, None] + i[None, None, :] - i[None, :, None]
    return _rel_bucket(rel).astype(jnp.int32)


def _attn_kernel(q_ref, k_ref, v_ref, bias_ref, lam_ref, g_ref, o_ref,
                 qz_ref, s_ref, p_ref, m_ref, l_ref, alpha_ref, acc_ref, *, lam_init, nk):
    t = ATT_TILE
    tq = ATT_QTILES * t
    qi = pl.program_id(2)
    qb = q_ref[0]
    lane = lax.broadcasted_iota(jnp.int32, (tq, 2 * ATT_HEAD_DIM), 1)
    qz_ref[0] = jnp.where(lane < ATT_HEAD_DIM, qb, jnp.zeros_like(qb))
    qz_ref[1] = jnp.where(lane >= ATT_HEAD_DIM, qb, jnp.zeros_like(qb))
    m_ref[...] = jnp.full(m_ref.shape, -jnp.inf, F32)
    l_ref[...] = jnp.zeros_like(l_ref)
    acc_ref[...] = jnp.zeros_like(acc_ref)
    half = N_BIAS_TILES // 2
    rb = ATT_ROWS

    def key_rows(kj):
        start = kj * t
        return pl.ds(start if isinstance(start, int) else pl.multiple_of(start, t), t)

    def apply_pv(slot, kj):
        vb = v_ref[0, key_rows(kj), :]
        for mp in range(2):
            acc_ref[mp] = alpha_ref[slot, mp] * acc_ref[mp] + _dot(p_ref[slot, mp], vb)

    def logits(slot, kj):
        kb = k_ref[0, key_rows(kj), :]
        for mp in range(2):
            s_ref[slot, mp] = _dot_nt(qz_ref[mp], kb)

    def softmax(slot, kj):
        tiles = [jnp.clip(kj - (ATT_QTILES * qi + u), -half, half) + half
                 for u in range(ATT_QTILES)]
        for mp in range(2):
            for r0 in range(0, tq, rb):
                rows = slice(r0, r0 + rb)
                bias_rows = slice(r0 % t, r0 % t + rb)
                s = s_ref[slot, mp, rows, :] + bias_ref[0, tiles[r0 // t], bias_rows, :]
                m_old = m_ref[mp, rows, :]
                m_new = jnp.maximum(m_old, jnp.max(s, axis=-1, keepdims=True))
                alpha = jnp.exp2(m_old - m_new)
                p = jnp.exp2(s - jnp.concatenate([m_new] * (t // LANES), axis=1))
                l_ref[mp, rows, :] = (alpha * l_ref[mp, rows, :]
                                      + jnp.sum(p, axis=-1, keepdims=True))
                m_ref[mp, rows, :] = m_new
                alpha_ref[slot, mp, rows, :] = alpha
                p_ref[slot, mp, rows, :] = p.astype(BF16)

    def pair(kk, carry, first=False):
        for slot in range(2):
            kj = 2 * kk + slot
            logits(slot, kj)
            if not (first and slot == 0):
                apply_pv(1 - slot, kj - 1)
            softmax(slot, kj)
        return carry

    pair(0, 0, first=True)
    lax.fori_loop(1, nk // 2, pair, 0)
    apply_pv(1, nk - 1)

    lp = lam_ref[...]
    lam = (jnp.exp(jnp.sum(lp[0:1] * lp[1:2], axis=-1, keepdims=True))
           - jnp.exp(jnp.sum(lp[2:3] * lp[3:4], axis=-1, keepdims=True)) + lam_init)
    o = acc_ref[0] / l_ref[0] - lam * (acc_ref[1] / l_ref[1])
    o_ref[0] = (_rms(o, g_ref[...]) * (1.0 - lam_init)).astype(o_ref.dtype)


def diff_attention(qkv, bias, lam_params, subln_g, lam_init):
    bsz, s, _ = qkv.shape
    t = ATT_TILE
    tq = ATT_QTILES * t
    nk = s // t
    hw = 2 * ATT_HEAD_DIM
    assert nk % 2 == 0 and s % tq == 0
    stat = pltpu.VMEM((2, tq, LANES), F32)
    return pl.pallas_call(
        functools.partial(_attn_kernel, lam_init=lam_init, nk=nk),
        grid=(bsz, ATT_HEADS, s // tq),
        in_specs=[
            pl.BlockSpec((1, tq, hw), lambda b, h, i: (b, i, h)),
            pl.BlockSpec((1, s, hw), lambda b, h, i: (b, 0, ATT_HEADS + h)),
            pl.BlockSpec((1, s, hw), lambda b, h, i: (b, 0, 2 * ATT_HEADS + h)),
            pl.BlockSpec((1, N_BIAS_TILES, t, t), lambda b, h, i: (h, 0, 0, 0)),
            pl.BlockSpec((4, ATT_HEAD_DIM), lambda b, h, i: (0, 0)),
            pl.BlockSpec((1, hw), lambda b, h, i: (0, 0)),
        ],
        out_specs=pl.BlockSpec((1, tq, hw), lambda b, h, i: (b, i, h)),
        out_shape=jax.ShapeDtypeStruct((bsz, s, ATT_WIDTH), BF16),
        scratch_shapes=[pltpu.VMEM((2, tq, hw), BF16),
                        pltpu.VMEM((2, 2, tq, t), F32),
                        pltpu.VMEM((2, 2, tq, t), BF16),
                        stat, stat,
                        pltpu.VMEM((2, 2, tq, LANES), F32),
                        pltpu.VMEM((2, tq, hw), F32)],
        compiler_params=_cparams(("parallel", "parallel", "parallel")),
        name="diff_attention",
    )(qkv, qkv, qkv, bias, lam_params, subln_g)


def _out_proj_kernel(yc_ref, ys_ref, ya_ref, h_ref, w_ref, o_ref):
    c0, c1 = CONV_CH, CONV_CH + SSD_WIDTH
    acc = _dot(yc_ref[...], w_ref[0:c0, :])
    acc = acc + _dot(ys_ref[...], w_ref[c0:c1, :])
    acc = acc + _dot(ya_ref[...], w_ref[c1:, :])
    o_ref[...] = h_ref[...] + acc


def out_proj(yc, ys, ya, h, w, tm=DENSE_TILE):
    m, d = h.shape
    row = lambda n: pl.BlockSpec((tm, n), lambda i: (i, 0))
    return pl.pallas_call(
        _out_proj_kernel,
        grid=(m // tm,),
        in_specs=[row(CONV_CH), row(SSD_WIDTH), row(ATT_WIDTH), row(d),
                  pl.BlockSpec(w.shape, lambda i: (0, 0), pipeline_mode=pl.Buffered(1))],
        out_specs=row(d),
        out_shape=jax.ShapeDtypeStruct((m, d), F32),
        compiler_params=_cparams(("parallel",)),
        name="out_proj",
    )(yc, ys, ya, h, w)


def _mlp_kernel(h_ref, g_ref, wu_ref, wd_ref, o_ref, xn_ref):
    f = pl.program_id(1)

    @pl.when(f == 0)
    def _():
        h = h_ref[...]
        xn_ref[...] = _rms(h, g_ref[...]).astype(BF16)
        o_ref[...] = h

    hid = jnp.square(jnp.maximum(_dot(xn_ref[...], wu_ref[...]), 0.0)).astype(BF16)
    o_ref[...] += _dot(hid, wd_ref[...])


def mlp(h, g, wu, wd, tm=DENSE_TILE, tf=FF_TILE):
    m, d = h.shape
    ff = wu.shape[1]
    return pl.pallas_call(
        _mlp_kernel,
        grid=(m // tm, ff // tf),
        in_specs=[pl.BlockSpec((tm, d), lambda i, f: (i, 0)),
                  pl.BlockSpec((1, d), lambda i, f: (0, 0)),
                  pl.BlockSpec((d, tf), lambda i, f: (0, f)),
                  pl.BlockSpec((tf, d), lambda i, f: (f, 0))],
        out_specs=pl.BlockSpec((tm, d), lambda i, f: (i, 0)),
        out_shape=jax.ShapeDtypeStruct((m, d), F32),
        scratch_shapes=[pltpu.VMEM((tm, d), BF16)],
        compiler_params=_cparams(("parallel", "arbitrary")),
        name="mlp",
    )(h, g, wu, wd)


def _ple_kernel(h_ref, p_ref, g_ref, wg_ref, wp_ref, gf_ref, o_ref, *, final):
    h = h_ref[...]
    gate = jax.nn.sigmoid(_dot(_rms(h, g_ref[...]).astype(BF16), wg_ref[...]))
    out = h + _dot(p_ref[...].astype(BF16), wp_ref[...]) * gate
    if final:
        out = _rms(out, gf_ref[...])
    o_ref[...] = out


def ple(h, p, g, wg, wp, gf, final, tm=DENSE_TILE):
    m, d = h.shape
    vec = pl.BlockSpec((1, d), lambda i: (0, 0))
    return pl.pallas_call(
        functools.partial(_ple_kernel, final=final),
        grid=(m // tm,),
        in_specs=[pl.BlockSpec((tm, d), lambda i: (i, 0)),
                  pl.BlockSpec((tm, PLE_DIM), lambda i: (i, 0)),
                  vec,
                  pl.BlockSpec((d, d), lambda i: (0, 0), pipeline_mode=pl.Buffered(1)),
                  pl.BlockSpec((PLE_DIM, d), lambda i: (0, 0), pipeline_mode=pl.Buffered(1)),
                  vec],
        out_specs=pl.BlockSpec((tm, d), lambda i: (i, 0)),
        out_shape=jax.ShapeDtypeStruct((m, d), F32),
        compiler_params=_cparams(("parallel",)),
        name="ple_final" if final else "ple",
    )(h, p, g, wg, wp, gf)


def _split_w_in_kernel(w_ref, oc_ref, os_ref, oa_ref):
    c_conv = 2 * CONV_CH
    c_att = c_conv + SSD_WIDTH + SSD_XBC + 2 * SSD_HEADS
    w = w_ref[0]
    oc_ref[...] = w[:, :c_conv].astype(BF16)
    body = SSD_SLAB - SSD_DT_COLS
    tail = w[:, c_conv + body:c_conv + body + SSD_DT_COLS]
    lane = lax.broadcasted_iota(jnp.int32, tail.shape, 1)
    tail = jnp.where(lane < 2 * SSD_HEADS, tail, 0.0)
    os_ref[...] = jnp.concatenate([w[:, c_conv:c_conv + body], tail], axis=1).astype(BF16)
    q = w[:, c_att:c_att + ATT_WIDTH] * (LOG2E * ATT_HEAD_DIM ** -0.5)
    oa_ref[...] = jnp.concatenate([q, w[:, c_att + ATT_WIDTH:]], axis=1).astype(BF16)


def split_w_in(w_in, layer):
    _, r, c = w_in.shape
    tr = SPLIT_ROWS
    assert r % tr == 0
    out = lambda n: pl.BlockSpec((tr, n), lambda j: (j, 0))
    return pl.pallas_call(
        _split_w_in_kernel,
        grid=(r // tr,),
        in_specs=[pl.BlockSpec((1, tr, c), lambda j: (layer, j, 0))],
        out_specs=[out(CONV_SLAB), out(SSD_SLAB), out(ATT_SLAB)],
        out_shape=[jax.ShapeDtypeStruct((r, CONV_SLAB), BF16),
                   jax.ShapeDtypeStruct((r, SSD_SLAB), BF16),
                   jax.ShapeDtypeStruct((r, ATT_SLAB), BF16)],
        compiler_params=_cparams(("parallel",)),
        name="split_w_in",
    )(w_in)


def _cast_kernel(w_ref, o_ref):
    o_ref[...] = w_ref[0].astype(o_ref.dtype)


def cast_layer(w, layer):
    _, r, c = w.shape
    tr = min(r, CAST_BLOCK_BYTES // (4 * c))
    assert r % tr == 0 and tr % SUBLANES == 0
    return pl.pallas_call(
        _cast_kernel,
        grid=(r // tr,),
        in_specs=[pl.BlockSpec((1, tr, c), lambda j: (layer, j, 0))],
        out_specs=pl.BlockSpec((tr, c), lambda j: (j, 0)),
        out_shape=jax.ShapeDtypeStruct((r, c), BF16),
        compiler_params=_cparams(("parallel",)),
        name="cast_layer",
    )(w)


def _expand_matrix(direction):
    e = np.zeros((LANES, SSD_WIDTH), np.float32)
    for hd in range(SSD_HEADS):
        e[direction * SSD_HEADS + hd, hd * SSD_HEAD_DIM:(hd + 1) * SSD_HEAD_DIM] = 1.0
    return jnp.asarray(e)


def _pad_lanes(v):
    return jnp.pad(v.reshape(1, -1), ((0, 0), (0, LANES - v.size)))


def _layer_params(i, norm_mix_g, w_in, conv_w, conv_b, conv_norm_g, conv_norm_b, ssd_conv_w,
                  ssd_conv_b, ssd_dt_bias, ssd_a_log, ssd_d, ssd_norm_g, lambda_q1, lambda_k1,
                  lambda_q2, lambda_k2, attn_subln_g, w_out, norm_mlp_g, w_up, w_down,
                  norm_ple_g, w_ple, w_ple_gate):
    row = lambda v: v.reshape(1, -1)
    w_conv, w_ssd, w_att = split_w_in(w_in, i)
    return dict(
        norm_mix_g=row(norm_mix_g[i]),
        w_conv=w_conv, w_ssd=w_ssd, w_att=w_att,
        conv_w=conv_w[i], conv_b=row(conv_b[i]), conv_g=row(conv_norm_g[i]),
        conv_beta=row(conv_norm_b[i]),
        ssd_cw=ssd_conv_w[i], ssd_cb=row(ssd_conv_b[i]),
        ssd_dtb=_pad_lanes(ssd_dt_bias[i]), ssd_alog=_pad_lanes(ssd_a_log[i]),
        ssd_dskip=row(jnp.repeat(ssd_d[i], SSD_HEAD_DIM)), ssd_ng=row(ssd_norm_g[i]),
        lam=jnp.stack([lambda_q1[i], lambda_k1[i], lambda_q2[i], lambda_k2[i]]),
        subln_g=row(attn_subln_g[i]),
        w_out=cast_layer(w_out, i),
        norm_mlp_g=row(norm_mlp_g[i]),
        w_up=cast_layer(w_up, i), w_down=cast_layer(w_down, i),
        norm_ple_g=row(norm_ple_g[i]),
        w_ple=cast_layer(w_ple, i), w_gate=cast_layer(w_ple_gate, i),
    )


def _run(x, p, layers, bias, expand_f, expand_b, final_g):
    bsz, s, d = x.shape
    m = bsz * s
    h = x.reshape(m, d)
    for i, lp in enumerate(layers):
        lam_init = 0.8 - 0.6 * math.exp(-0.3 * i)
        ps, pa, y_conv = proj_conv(h, lp["norm_mix_g"], lp["w_conv"], lp["w_ssd"], lp["w_att"],
                                   lp["conv_w"], lp["conv_b"], lp["conv_g"], lp["conv_beta"], s)
        ps = ps.reshape(bsz, s, SSD_SLAB)
        pa = pa.reshape(bsz, s, ATT_SLAB)
        y_ssd = ssd_mixer(ps, lp["ssd_cw"], lp["ssd_cb"], lp["ssd_dtb"], lp["ssd_alog"],
                          expand_f, expand_b, lp["ssd_dskip"], lp["ssd_ng"])
        y_att = diff_attention(pa, bias, lp["lam"], lp["subln_g"], lam_init)
        h = out_proj(y_conv, y_ssd.reshape(m, SSD_WIDTH),
                     y_att.reshape(m, ATT_WIDTH), h, lp["w_out"])
        h = mlp(h, lp["norm_mlp_g"], lp["w_up"], lp["w_down"])
        h = ple(h, p[i].reshape(m, PLE_DIM), lp["norm_ple_g"], lp["w_gate"], lp["w_ple"],
                final_g, final=(i == len(layers) - 1))
    return h.reshape(bsz, s, d)


def kernel(x_prompt, x_sample, p_prompt, p_sample, norm_mix_g, w_in, conv_w, conv_b, conv_norm_g,
           conv_norm_b, ssd_conv_w, ssd_conv_b, ssd_dt_bias, ssd_a_log, ssd_d, ssd_norm_g,
           lambda_q1, lambda_k1, lambda_q2, lambda_k2, attn_subln_g, rel_bias, w_out, norm_mlp_g,
           w_up, w_down, norm_ple_g, w_ple, w_ple_gate, final_norm_g):
    layers = [_layer_params(i, norm_mix_g, w_in, conv_w, conv_b, conv_norm_g, conv_norm_b,
                            ssd_conv_w, ssd_conv_b, ssd_dt_bias, ssd_a_log, ssd_d, ssd_norm_g,
                            lambda_q1, lambda_k1, lambda_q2, lambda_k2, attn_subln_g, w_out,
                            norm_mlp_g, w_up, w_down, norm_ple_g, w_ple, w_ple_gate)
              for i in range(DEPTH)]
    bias = bias_tiles(_bucket_tiles(ATT_TILE), rel_bias)
    expand_f, expand_b = _expand_matrix(0), _expand_matrix(1)
    final_g = final_norm_g.reshape(1, -1)
    y_prompt = _run(x_prompt, p_prompt, layers, bias, expand_f, expand_b, final_g)
    y_sample = _run(x_sample, p_sample, layers, bias, expand_f, expand_b, final_g)
    return (y_prompt, y_sample)
```

```python
import functools
import math

import jax
import jax.numpy as jnp
import numpy as np
from jax import lax
from jax.experimental import pallas as pl
from jax.experimental.pallas import tpu as pltpu

F32 = jnp.float32
BF16 = jnp.bfloat16

D_MODEL = 2048
DEPTH = 2
PLE_DIM = 256
EPS = 1e-6
CONV_CH = 512
CONV_WIDTH = 31
CONV_PAD = (CONV_WIDTH - 1) // 2
SSD_HEAD_DIM = 64
SSD_HEADS = 12
SSD_WIDTH = SSD_HEADS * SSD_HEAD_DIM
SSD_GROUPS = 2
SSD_HPG = SSD_HEADS // SSD_GROUPS
SSD_STATE = 128
SSD_CONV_WIDTH = 5
SSD_CONV_PAD = (SSD_CONV_WIDTH - 1) // 2
SSD_CHUNK = 128
SSD_XBC = SSD_WIDTH + 2 * SSD_GROUPS * SSD_STATE
ATT_HEADS = 6
ATT_HEAD_DIM = 64
ATT_WIDTH = ATT_HEADS * 2 * ATT_HEAD_DIM
REL_BUCKETS = 32
REL_MAX_DIST = 128
D_FF = 4 * D_MODEL
LOG2E = math.log2(math.e)

LANES = 128
SUBLANES = 8
VMEM_LIMIT = 56 * 1024 * 1024

SSD_DT_COLS = LANES
SSD_SLAB = SSD_WIDTH + SSD_XBC + SSD_DT_COLS
CONV_SLAB = 2 * CONV_CH
ATT_SLAB = 3 * ATT_WIDTH
SSD_GW = SSD_HPG * SSD_HEAD_DIM

ATT_TILE = 512
ATT_QTILES = 2
N_BIAS_TILES = 5
ATT_ROWS = 32
CONV_TILE = 128
CONV_HALO = 16
CONV_ROWS = 32
SSD_HALO = SUBLANES
SSD_BATCH = 4
ROW_TILE = 512
DENSE_TILE = 1024
FF_TILE = 512
CAST_BLOCK_BYTES = 8 * 1024 * 1024
SPLIT_ROWS = 256


def _cparams(sem):
    return pltpu.CompilerParams(dimension_semantics=sem, vmem_limit_bytes=VMEM_LIMIT)


def _rms(x, g):
    ms = jnp.mean(x * x, axis=-1, keepdims=True)
    return (x * lax.rsqrt(ms + EPS)) * g


def _silu(x):
    return x * jax.nn.sigmoid(x)


def _dot(a, b):
    return jnp.dot(a, b, preferred_element_type=F32)


def _split3(x):
    hi = x.astype(BF16)
    r1 = x - hi.astype(F32)
    mid = r1.astype(BF16)
    lo = (r1 - mid.astype(F32)).astype(BF16)
    return hi, mid, lo


def _select_dot(sel, x):
    sel = sel.astype(BF16)
    hi, mid, lo = _split3(x)
    return _dot(sel, hi) + (_dot(sel, mid) + _dot(sel, lo))


def _dot_select(x, sel):
    sel = sel.astype(BF16)
    hi = x.astype(BF16)
    mid = (x - hi.astype(F32)).astype(BF16)
    return _dot(hi, sel) + _dot(mid, sel)


def _dot_nt(a, b):
    return lax.dot_general(a, b, (((1,), (1,)), ((), ())), preferred_element_type=F32)


def _conv_tile(buf_ref, shift_ref, w, b_ref, g_ref, beta_ref, o_ref, row0):
    span = CONV_TILE + 2 * CONV_HALO - SUBLANES
    for b in range(1, SUBLANES):
        shift_ref[b - 1] = buf_ref[b:b + span, :]
    for r0 in range(0, CONV_TILE, CONV_ROWS):
        acc = jnp.broadcast_to(b_ref[...], (CONV_ROWS, CONV_CH))
        for k in range(CONV_WIDTH):
            start = r0 + CONV_HALO - CONV_PAD + k
            b, base = start % SUBLANES, start - start % SUBLANES
            window = (buf_ref[base:base + CONV_ROWS, :] if b == 0
                      else shift_ref[b - 1, base:base + CONV_ROWS, :])
            acc = acc + w[k:k + 1, :] * window
        mu = jnp.mean(acc, axis=-1, keepdims=True)
        xc = acc - mu
        var = jnp.mean(xc * xc, axis=-1, keepdims=True)
        y = xc * lax.rsqrt(var + EPS) * g_ref[...] + beta_ref[...]
        o_ref[row0 + r0:row0 + r0 + CONV_ROWS, :] = _silu(y).astype(o_ref.dtype)


def _proj_conv_kernel(x_ref, g_ref, wc_ref, ws_ref, wa_ref, cw_ref, cb_ref, cg_ref, cbeta_ref,
                      os_ref, oa_ref, oy_ref, ubuf_ref, unext_ref, buf_ref, shift_ref,
                      *, tiles_per_seq):
    tm, halo = ROW_TILE, CONV_HALO
    i = pl.program_id(0)

    @pl.when(i == 0)
    def _():
        ubuf_ref[...] = jnp.zeros_like(ubuf_ref)
        unext_ref[...] = jnp.zeros_like(unext_ref)

    ubuf_ref[0:halo, :] = ubuf_ref[tm:tm + halo, :]
    ubuf_ref[halo:halo + tm, :] = unext_ref[...]

    xn = _rms(x_ref[...], g_ref[...]).astype(BF16)
    pc = _dot(xn, wc_ref[...])
    u = pc[:, :CONV_CH] * jax.nn.sigmoid(pc[:, CONV_CH:])
    unext_ref[...] = u
    ubuf_ref[halo + tm:, :] = u[0:halo, :]
    os_ref[...] = _dot(xn, ws_ref[...])
    oa_ref[...] = _dot(xn, wa_ref[...]).astype(oa_ref.dtype)

    pos = lax.rem(i + tiles_per_seq - 1, tiles_per_seq)
    w = cw_ref[...]
    n_sub = tm // CONV_TILE
    for st in range(n_sub):
        base = st * CONV_TILE
        buf_ref[...] = ubuf_ref[base:base + CONV_TILE + 2 * halo, :]
        if st == 0:
            buf_ref[0:halo, :] = jnp.where(pos > 0, ubuf_ref[0:halo, :], 0.0)
        if st == n_sub - 1:
            buf_ref[halo + CONV_TILE:, :] = jnp.where(pos < tiles_per_seq - 1,
                                                      ubuf_ref[halo + tm:, :], 0.0)
        _conv_tile(buf_ref, shift_ref, w, cb_ref, cg_ref, cbeta_ref, oy_ref, base)


def proj_conv(x, g, w_conv, w_ssd, w_att, cw, cb, cg, cbeta, seq_len, tm=ROW_TILE):
    m, d = x.shape
    assert seq_len % tm == 0 and m % seq_len == 0
    n = m // tm
    resident = lambda w: pl.BlockSpec(w.shape, lambda i: (0, 0), pipeline_mode=pl.Buffered(1))
    cur = lambda cols: pl.BlockSpec((tm, cols), lambda i: (jnp.minimum(i, n - 1), 0))
    vec = lambda: pl.BlockSpec((1, CONV_CH), lambda i: (0, 0))
    return pl.pallas_call(
        functools.partial(_proj_conv_kernel, tiles_per_seq=seq_len // tm),
        grid=(n + 1,),
        in_specs=[cur(d), pl.BlockSpec((1, d), lambda i: (0, 0)),
                  resident(w_conv), resident(w_ssd), resident(w_att),
                  pl.BlockSpec((CONV_WIDTH, CONV_CH), lambda i: (0, 0)), vec(), vec(), vec()],
        out_specs=[cur(SSD_SLAB), cur(ATT_SLAB),
                   pl.BlockSpec((tm, CONV_CH), lambda i: (jnp.maximum(i - 1, 0), 0))],
        out_shape=[jax.ShapeDtypeStruct((m, SSD_SLAB), F32),
                   jax.ShapeDtypeStruct((m, ATT_SLAB), BF16),
                   jax.ShapeDtypeStruct((m, CONV_CH), BF16)],
        scratch_shapes=[pltpu.VMEM((tm + 2 * CONV_HALO, CONV_CH), F32),
                        pltpu.VMEM((tm, CONV_CH), F32),
                        pltpu.VMEM((CONV_TILE + 2 * CONV_HALO, CONV_CH), F32),
                        pltpu.VMEM((SUBLANES - 1, CONV_TILE + 2 * CONV_HALO - SUBLANES, CONV_CH),
                                   F32)],
        compiler_params=_cparams(("arbitrary",)),
        name="proj_conv",
    )(x, g, w_conv, w_ssd, w_att, cw, cb, cg, cbeta)


def _ssd_chunk(xbc, dt_raw, dtb, alog, expand, state_ref, bi, *, reverse):
    q = SSD_CHUNK
    xs = xbc[:, :SSD_WIDTH]
    bmat = xbc[:, SSD_WIDTH:SSD_WIDTH + SSD_GROUPS * SSD_STATE]
    cmat = xbc[:, SSD_WIDTH + SSD_GROUPS * SSD_STATE:]

    dt = jax.nn.softplus(dt_raw + dtb)
    dta = dt * (-jnp.exp(alog))
    row = lax.broadcasted_iota(jnp.int32, (q, q), 0)
    col = lax.broadcasted_iota(jnp.int32, (q, q), 1)
    keep = (row <= col) if reverse else (row >= col)
    a_cs = _select_dot(keep, dta)
    a_tot = jnp.sum(dta, axis=0, keepdims=True)
    ea = jnp.exp(a_cs)
    dec = jnp.exp(a_tot - a_cs)
    cdec = jnp.broadcast_to(jnp.exp(a_tot), (SUBLANES, LANES))
    ex = _dot_select(jnp.concatenate([dt, ea, dec, cdec], axis=0), expand)
    dtx, eax, decx, cdecx = ex[0:q], ex[q:2 * q], ex[2 * q:3 * q], ex[3 * q:3 * q + 1]
    xdt = xs * dtx
    xdt_b = xdt.astype(BF16)
    xd_b = (xdt * decx).astype(BF16)
    a_cs_t = a_cs.T
    lane = lax.broadcasted_iota(jnp.int32, (q, LANES), 1)
    lo_half = lane < SSD_HEAD_DIM
    lane0 = SSD_HEADS if reverse else 0

    y_groups = []
    for g in range(SSD_GROUPS):
        gs = slice(g * SSD_GW, (g + 1) * SSD_GW)
        bg = bmat[:, g * SSD_STATE:(g + 1) * SSD_STATE]
        cg_b = cmat[:, g * SSD_STATE:(g + 1) * SSD_STATE].astype(BF16)
        cb = _dot_nt(cg_b, bg.astype(BF16))
        prev = state_ref[bi, :, gs]
        y_off = _dot(cg_b, prev.astype(BF16)) * eax[:, gs]
        new_states = _dot(bg.T.astype(BF16), xd_b[:, gs])
        state_ref[bi, :, gs] = prev * cdecx[:, gs] + new_states
        pairs = []
        for j in range(SSD_HPG // 2):
            xpair = xdt_b[:, g * SSD_GW + j * LANES:g * SSD_GW + (j + 1) * LANES]
            ypair = None
            for half in range(2):
                ln = lane0 + g * SSD_HPG + 2 * j + half
                seg = a_cs[:, ln:ln + 1] - a_cs_t[ln:ln + 1, :]
                decay = jnp.exp(jnp.where(keep, seg, -jnp.inf))
                mixer = (cb * decay).astype(BF16)
                xh = jnp.where(lo_half if half == 0 else jnp.logical_not(lo_half), xpair,
                               jnp.zeros_like(xpair))
                part = _dot(mixer, xh)
                ypair = part if ypair is None else ypair + part
            pairs.append(ypair)
        y_groups.append(jnp.concatenate(pairs, axis=1) + y_off)
    return jnp.concatenate(y_groups, axis=1), xs


def _ssd_fwd_kernel(cur_ref, left_ref, right_ref, cw_ref, cb_ref, dtb_ref, alog_ref, exp_ref,
                    y_ref, xc_ref, state_ref, buf_ref):
    q = SSD_CHUNK
    c = pl.program_id(1)
    nc = pl.num_programs(1)

    @pl.when(c == 0)
    def _():
        state_ref[...] = jnp.zeros_like(state_ref)

    xbc_lo, xbc_hi = SSD_WIDTH, SSD_WIDTH + SSD_XBC
    cw = cw_ref[...]
    for bi in range(SSD_BATCH):
        blk = cur_ref[bi]
        buf_ref[bi, 0:SSD_HALO, :] = jnp.where(c > 0, left_ref[bi][:, xbc_lo:xbc_hi], 0.0)
        buf_ref[bi, SSD_HALO:SSD_HALO + q, :] = blk[:, xbc_lo:xbc_hi]
        buf_ref[bi, SSD_HALO + q:, :] = jnp.where(c < nc - 1, right_ref[bi][:, xbc_lo:xbc_hi], 0.0)
        acc = jnp.broadcast_to(cb_ref[...], (q, SSD_XBC))
        for k in range(SSD_CONV_WIDTH):
            start = SSD_HALO - SSD_CONV_PAD + k
            acc = acc + cw[k:k + 1, :] * buf_ref[bi, start:start + q, :]
        xbc = _silu(acc)
        xc_ref[bi] = xbc
        y, _ = _ssd_chunk(xbc, blk[:, xbc_hi:], dtb_ref[...], alog_ref[...], exp_ref[...],
                          state_ref, bi, reverse=False)
        y_ref[bi] = y


def _ssd_bwd_kernel(xc_ref, dt_ref, z_ref, yf_ref, dtb_ref, alog_ref, exp_ref, dskip_ref, ng_ref,
                    o_ref, state_ref):
    c = pl.program_id(1)

    @pl.when(c == 0)
    def _():
        state_ref[...] = jnp.zeros_like(state_ref)

    for bi in range(SSD_BATCH):
        y, xs = _ssd_chunk(xc_ref[bi], dt_ref[bi], dtb_ref[...], alog_ref[...], exp_ref[...],
                           state_ref, bi, reverse=True)
        y = y + yf_ref[bi] + dskip_ref[...] * xs
        o_ref[bi] = _rms(y * _silu(z_ref[bi]), ng_ref[...]).astype(o_ref.dtype)


def ssd_mixer(ps, cw, cb, dtb, alog, expand_f, expand_b, dskip, ng):
    bsz, s, _ = ps.shape
    q, nb = SSD_CHUNK, SSD_BATCH
    assert bsz % nb == 0 and s % q == 0
    nc = s // q
    hb = q // SSD_HALO
    last_halo = s // SSD_HALO - 1
    full = lambda shape: pl.BlockSpec(shape, lambda i, c: (0,) * len(shape))
    params = [full((1, LANES)), full((1, LANES)), full((LANES, SSD_WIDTH))]
    state = pltpu.VMEM((nb, SSD_STATE, SSD_WIDTH), F32)
    fwd = lambda width, col=0: pl.BlockSpec((nb, q, width), lambda i, c: (i, c, col))
    y_f, xc = pl.pallas_call(
        _ssd_fwd_kernel,
        grid=(bsz // nb, nc),
        in_specs=[fwd(SSD_SLAB),
                  pl.BlockSpec((nb, SSD_HALO, SSD_SLAB),
                               lambda i, c: (i, jnp.maximum(c * hb - 1, 0), 0)),
                  pl.BlockSpec((nb, SSD_HALO, SSD_SLAB),
                               lambda i, c: (i, jnp.minimum((c + 1) * hb, last_halo), 0)),
                  full((SSD_CONV_WIDTH, SSD_XBC)), full((1, SSD_XBC))] + params,
        out_specs=[fwd(SSD_WIDTH), fwd(SSD_XBC)],
        out_shape=[jax.ShapeDtypeStruct((bsz, s, SSD_WIDTH), F32),
                   jax.ShapeDtypeStruct((bsz, s, SSD_XBC), F32)],
        scratch_shapes=[state, pltpu.VMEM((nb, q + 2 * SSD_HALO, SSD_XBC), F32)],
        compiler_params=_cparams(("parallel", "arbitrary")),
        name="ssd_fwd",
    )(ps, ps, ps, cw, cb, dtb, alog, expand_f)
    bwd = lambda width, col=0: pl.BlockSpec((nb, q, width), lambda i, c: (i, nc - 1 - c, col))
    dt_col = (SSD_WIDTH + SSD_XBC) // SSD_DT_COLS
    return pl.pallas_call(
        _ssd_bwd_kernel,
        grid=(bsz // nb, nc),
        in_specs=[bwd(SSD_XBC), bwd(SSD_DT_COLS, dt_col), bwd(SSD_WIDTH), bwd(SSD_WIDTH)]
                 + params + [full((1, SSD_WIDTH)), full((1, SSD_WIDTH))],
        out_specs=bwd(SSD_WIDTH),
        out_shape=jax.ShapeDtypeStruct((bsz, s, SSD_WIDTH), BF16),
        scratch_shapes=[state],
        compiler_params=_cparams(("parallel", "arbitrary")),
        name="ssd_bwd",
    )(xc, ps, ps, y_f, dtb, alog, expand_b, dskip, ng)


def _bias_tiles_kernel(bucket_ref, table_ref, o_ref):
    h = pl.program_id(0)
    bucket = bucket_ref[0]
    out = jnp.zeros(bucket.shape, F32)
    for b in range(REL_BUCKETS):
        out = jnp.where(bucket == b, table_ref[b, h] * LOG2E, out)
    o_ref[0, 0] = out


def bias_tiles(buckets, rel_bias):
    nt, t, _ = buckets.shape
    return pl.pallas_call(
        _bias_tiles_kernel,
        grid=(ATT_HEADS, nt),
        in_specs=[pl.BlockSpec((1, t, t), lambda h, d: (d, 0, 0)),
                  pl.BlockSpec(memory_space=pltpu.SMEM)],
        out_specs=pl.BlockSpec((1, 1, t, t), lambda h, d: (h, d, 0, 0)),
        out_shape=jax.ShapeDtypeStruct((ATT_HEADS, nt, t, t), F32),
        compiler_params=_cparams(("parallel", "parallel")),
        name="bias_tiles",
    )(buckets, rel_bias)


def _rel_bucket(rel):
    nb = REL_BUCKETS // 2
    max_exact = nb // 2
    ret = jnp.where(rel > 0, nb, 0)
    n = jnp.abs(rel)
    nf = jnp.maximum(n, 1).astype(jnp.float32)
    large = max_exact + (jnp.log(nf / max_exact) / math.log(REL_MAX_DIST / max_exact)
                         * (nb - max_exact)).astype(jnp.int32)
    large = jnp.minimum(large, nb - 1)
    return ret + jnp.where(n < max_exact, n, large)


def _bucket_tiles(t):
    i = jnp.arange(t, dtype=jnp.int32)
    half = N_BIAS_TILES // 2
    offs = (jnp.arange(N_BIAS_TILES, dtype=jnp.int32) - half) * t
    rel = offs[:, None, None] + i[None, None, :] - i[None, :, None]
    return _rel_bucket(rel).astype(jnp.int32)


def _attn_kernel(q_ref, k_ref, v_ref, bias_ref, lam_ref, g_ref, o_ref,
                 qz_ref, s_ref, p_ref, m_ref, l_ref, alpha_ref, acc_ref, *, lam_init, nk):
    t = ATT_TILE
    tq = ATT_QTILES * t
    qi = pl.program_id(2)
    qb = q_ref[0]
    lane = lax.broadcasted_iota(jnp.int32, (tq, 2 * ATT_HEAD_DIM), 1)
    qz_ref[0] = jnp.where(lane < ATT_HEAD_DIM, qb, jnp.zeros_like(qb))
    qz_ref[1] = jnp.where(lane >= ATT_HEAD_DIM, qb, jnp.zeros_like(qb))
    m_ref[...] = jnp.full(m_ref.shape, -jnp.inf, F32)
    l_ref[...] = jnp.zeros_like(l_ref)
    acc_ref[...] = jnp.zeros_like(acc_ref)
    half = N_BIAS_TILES // 2
    rb = ATT_ROWS

    def key_rows(kj):
        start = kj * t
        return pl.ds(start if isinstance(start, int) else pl.multiple_of(start, t), t)

    def apply_pv(slot, kj):
        vb = v_ref[0, key_rows(kj), :]
        for mp in range(2):
            acc_ref[mp] = alpha_ref[slot, mp] * acc_ref[mp] + _dot(p_ref[slot, mp], vb)

    def logits(slot, kj):
        kb = k_ref[0, key_rows(kj), :]
        for mp in range(2):
            s_ref[slot, mp] = _dot_nt(qz_ref[mp], kb)

    def softmax(slot, kj):
        tiles = [jnp.clip(kj - (ATT_QTILES * qi + u), -half, half) + half
                 for u in range(ATT_QTILES)]
        for mp in range(2):
            for r0 in range(0, tq, rb):
                rows = slice(r0, r0 + rb)
                bias_rows = slice(r0 % t, r0 % t + rb)
                s = s_ref[slot, mp, rows, :] + bias_ref[0, tiles[r0 // t], bias_rows, :]
                m_old = m_ref[mp, rows, :]
                m_new = jnp.maximum(m_old, jnp.max(s, axis=-1, keepdims=True))
                alpha = jnp.exp2(m_old - m_new)
                p = jnp.exp2(s - jnp.concatenate([m_new] * (t // LANES), axis=1))
                l_ref[mp, rows, :] = (alpha * l_ref[mp, rows, :]
                                      + jnp.sum(p, axis=-1, keepdims=True))
                m_ref[mp, rows, :] = m_new
                alpha_ref[slot, mp, rows, :] = alpha
                p_ref[slot, mp, rows, :] = p.astype(BF16)

    def pair(kk, carry, first=False):
        for slot in range(2):
            kj = 2 * kk + slot
            logits(slot, kj)
            if not (first and slot == 0):
                apply_pv(1 - slot, kj - 1)
            softmax(slot, kj)
        return carry

    pair(0, 0, first=True)
    lax.fori_loop(1, nk // 2, pair, 0)
    apply_pv(1, nk - 1)

    lp = lam_ref[...]
    lam = (jnp.exp(jnp.sum(lp[0:1] * lp[1:2], axis=-1, keepdims=True))
           - jnp.exp(jnp.sum(lp[2:3] * lp[3:4], axis=-1, keepdims=True)) + lam_init)
    o = acc_ref[0] / l_ref[0] - lam * (acc_ref[1] / l_ref[1])
    o_ref[0] = (_rms(o, g_ref[...]) * (1.0 - lam_init)).astype(o_ref.dtype)


def diff_attention(qkv, bias, lam_params, subln_g, lam_init):
    bsz, s, _ = qkv.shape
    t = ATT_TILE
    tq = ATT_QTILES * t
    nk = s // t
    hw = 2 * ATT_HEAD_DIM
    assert nk % 2 == 0 and s % tq == 0
    stat = pltpu.VMEM((2, tq, LANES), F32)
    return pl.pallas_call(
        functools.partial(_attn_kernel, lam_init=lam_init, nk=nk),
        grid=(bsz, ATT_HEADS, s // tq),
        in_specs=[
            pl.BlockSpec((1, tq, hw), lambda b, h, i: (b, i, h)),
            pl.BlockSpec((1, s, hw), lambda b, h, i: (b, 0, ATT_HEADS + h)),
            pl.BlockSpec((1, s, hw), lambda b, h, i: (b, 0, 2 * ATT_HEADS + h)),
            pl.BlockSpec((1, N_BIAS_TILES, t, t), lambda b, h, i: (h, 0, 0, 0)),
            pl.BlockSpec((4, ATT_HEAD_DIM), lambda b, h, i: (0, 0)),
            pl.BlockSpec((1, hw), lambda b, h, i: (0, 0)),
        ],
        out_specs=pl.BlockSpec((1, tq, hw), lambda b, h, i: (b, i, h)),
        out_shape=jax.ShapeDtypeStruct((bsz, s, ATT_WIDTH), BF16),
        scratch_shapes=[pltpu.VMEM((2, tq, hw), BF16),
                        pltpu.VMEM((2, 2, tq, t), F32),
                        pltpu.VMEM((2, 2, tq, t), BF16),
                        stat, stat,
                        pltpu.VMEM((2, 2, tq, LANES), F32),
                        pltpu.VMEM((2, tq, hw), F32)],
        compiler_params=_cparams(("parallel", "parallel", "parallel")),
        name="diff_attention",
    )(qkv, qkv, qkv, bias, lam_params, subln_g)


def _out_proj_kernel(yc_ref, ys_ref, ya_ref, h_ref, w_ref, o_ref):
    c0, c1 = CONV_CH, CONV_CH + SSD_WIDTH
    acc = _dot(yc_ref[...], w_ref[0:c0, :])
    acc = acc + _dot(ys_ref[...], w_ref[c0:c1, :])
    acc = acc + _dot(ya_ref[...], w_ref[c1:, :])
    o_ref[...] = h_ref[...] + acc


def out_proj(yc, ys, ya, h, w, tm=DENSE_TILE):
    m, d = h.shape
    row = lambda n: pl.BlockSpec((tm, n), lambda i: (i, 0))
    return pl.pallas_call(
        _out_proj_kernel,
        grid=(m // tm,),
        in_specs=[row(CONV_CH), row(SSD_WIDTH), row(ATT_WIDTH), row(d),
                  pl.BlockSpec(w.shape, lambda i: (0, 0), pipeline_mode=pl.Buffered(1))],
        out_specs=row(d),
        out_shape=jax.ShapeDtypeStruct((m, d), F32),
        compiler_params=_cparams(("parallel",)),
        name="out_proj",
    )(yc, ys, ya, h, w)


def _mlp_kernel(h_ref, g_ref, wu_ref, wd_ref, o_ref, xn_ref):
    f = pl.program_id(1)

    @pl.when(f == 0)
    def _():
        h = h_ref[...]
        xn_ref[...] = _rms(h, g_ref[...]).astype(BF16)
        o_ref[...] = h

    hid = jnp.square(jnp.maximum(_dot(xn_ref[...], wu_ref[...]), 0.0)).astype(BF16)
    o_ref[...] += _dot(hid, wd_ref[...])


def mlp(h, g, wu, wd, tm=DENSE_TILE, tf=FF_TILE):
    m, d = h.shape
    ff = wu.shape[1]
    return pl.pallas_call(
        _mlp_kernel,
        grid=(m // tm, ff // tf),
        in_specs=[pl.BlockSpec((tm, d), lambda i, f: (i, 0)),
                  pl.BlockSpec((1, d), lambda i, f: (0, 0)),
                  pl.BlockSpec((d, tf), lambda i, f: (0, f)),
                  pl.BlockSpec((tf, d), lambda i, f: (f, 0))],
        out_specs=pl.BlockSpec((tm, d), lambda i, f: (i, 0)),
        out_shape=jax.ShapeDtypeStruct((m, d), F32),
        scratch_shapes=[pltpu.VMEM((tm, d), BF16)],
        compiler_params=_cparams(("parallel", "arbitrary")),
        name="mlp",
    )(h, g, wu, wd)


def _ple_kernel(h_ref, p_ref, g_ref, wg_ref, wp_ref, gf_ref, o_ref, *, final):
    h = h_ref[...]
    gate = jax.nn.sigmoid(_dot(_rms(h, g_ref[...]).astype(BF16), wg_ref[...]))
    out = h + _dot(p_ref[0].astype(BF16), wp_ref[...]) * gate
    if final:
        out = _rms(out, gf_ref[...])
    o_ref[...] = out


def ple(h, p, layer, g, wg, wp, gf, final, tm=DENSE_TILE):
    m, d = h.shape
    vec = pl.BlockSpec((1, d), lambda i: (0, 0))
    return pl.pallas_call(
        functools.partial(_ple_kernel, final=final),
        grid=(m // tm,),
        in_specs=[pl.BlockSpec((tm, d), lambda i: (i, 0)),
                  pl.BlockSpec((1, tm, PLE_DIM), lambda i: (layer, i, 0)),
                  vec,
                  pl.BlockSpec((d, d), lambda i: (0, 0), pipeline_mode=pl.Buffered(1)),
                  pl.BlockSpec((PLE_DIM, d), lambda i: (0, 0), pipeline_mode=pl.Buffered(1)),
                  vec],
        out_specs=pl.BlockSpec((tm, d), lambda i: (i, 0)),
        out_shape=jax.ShapeDtypeStruct((m, d), F32),
        compiler_params=_cparams(("parallel",)),
        name="ple_final" if final else "ple",
    )(h, p, g, wg, wp, gf)


def _split_w_in_kernel(w_ref, oc_ref, os_ref, oa_ref):
    c_conv = 2 * CONV_CH
    c_att = c_conv + SSD_WIDTH + SSD_XBC + 2 * SSD_HEADS
    w = w_ref[0]
    oc_ref[...] = w[:, :c_conv].astype(BF16)
    body = SSD_SLAB - SSD_DT_COLS
    tail = w[:, c_conv + body:c_conv + body + SSD_DT_COLS]
    lane = lax.broadcasted_iota(jnp.int32, tail.shape, 1)
    tail = jnp.where(lane < 2 * SSD_HEADS, tail, 0.0)
    os_ref[...] = jnp.concatenate([w[:, c_conv:c_conv + body], tail], axis=1).astype(BF16)
    q = w[:, c_att:c_att + ATT_WIDTH] * (LOG2E * ATT_HEAD_DIM ** -0.5)
    oa_ref[...] = jnp.concatenate([q, w[:, c_att + ATT_WIDTH:]], axis=1).astype(BF16)


def split_w_in(w_in, layer):
    _, r, c = w_in.shape
    tr = SPLIT_ROWS
    assert r % tr == 0
    out = lambda n: pl.BlockSpec((tr, n), lambda j: (j, 0))
    return pl.pallas_call(
        _split_w_in_kernel,
        grid=(r // tr,),
        in_specs=[pl.BlockSpec((1, tr, c), lambda j: (layer, j, 0))],
        out_specs=[out(CONV_SLAB), out(SSD_SLAB), out(ATT_SLAB)],
        out_shape=[jax.ShapeDtypeStruct((r, CONV_SLAB), BF16),
                   jax.ShapeDtypeStruct((r, SSD_SLAB), BF16),
                   jax.ShapeDtypeStruct((r, ATT_SLAB), BF16)],
        compiler_params=_cparams(("parallel",)),
        name="split_w_in",
    )(w_in)


def _cast_kernel(w_ref, o_ref):
    o_ref[...] = w_ref[0].astype(o_ref.dtype)


def cast_layer(w, layer):
    _, r, c = w.shape
    tr = min(r, CAST_BLOCK_BYTES // (4 * c))
    assert r % tr == 0 and tr % SUBLANES == 0
    return pl.pallas_call(
        _cast_kernel,
        grid=(r // tr,),
        in_specs=[pl.BlockSpec((1, tr, c), lambda j: (layer, j, 0))],
        out_specs=pl.BlockSpec((tr, c), lambda j: (j, 0)),
        out_shape=jax.ShapeDtypeStruct((r, c), BF16),
        compiler_params=_cparams(("parallel",)),
        name="cast_layer",
    )(w)


def _expand_matrix(direction):
    e = np.zeros((LANES, SSD_WIDTH), np.float32)
    for hd in range(SSD_HEADS):
        e[direction * SSD_HEADS + hd, hd * SSD_HEAD_DIM:(hd + 1) * SSD_HEAD_DIM] = 1.0
    return jnp.asarray(e)


def _pad_lanes(v):
    return jnp.pad(v.reshape(1, -1), ((0, 0), (0, LANES - v.size)))


def _layer_params(i, norm_mix_g, w_in, conv_w, conv_b, conv_norm_g, conv_norm_b, ssd_conv_w,
                  ssd_conv_b, ssd_dt_bias, ssd_a_log, ssd_d, ssd_norm_g, lambda_q1, lambda_k1,
                  lambda_q2, lambda_k2, attn_subln_g, w_out, norm_mlp_g, w_up, w_down,
                  norm_ple_g, w_ple, w_ple_gate):
    row = lambda v: v.reshape(1, -1)
    w_conv, w_ssd, w_att = split_w_in(w_in, i)
    return dict(
        norm_mix_g=row(norm_mix_g[i]),
        w_conv=w_conv, w_ssd=w_ssd, w_att=w_att,
        conv_w=conv_w[i], conv_b=row(conv_b[i]), conv_g=row(conv_norm_g[i]),
        conv_beta=row(conv_norm_b[i]),
        ssd_cw=ssd_conv_w[i], ssd_cb=row(ssd_conv_b[i]),
        ssd_dtb=_pad_lanes(ssd_dt_bias[i]), ssd_alog=_pad_lanes(ssd_a_log[i]),
        ssd_dskip=row(jnp.repeat(ssd_d[i], SSD_HEAD_DIM)), ssd_ng=row(ssd_norm_g[i]),
        lam=jnp.stack([lambda_q1[i], lambda_k1[i], lambda_q2[i], lambda_k2[i]]),
        subln_g=row(attn_subln_g[i]),
        w_out=cast_layer(w_out, i),
        norm_mlp_g=row(norm_mlp_g[i]),
        w_up=cast_layer(w_up, i), w_down=cast_layer(w_down, i),
        norm_ple_g=row(norm_ple_g[i]),
        w_ple=cast_layer(w_ple, i), w_gate=cast_layer(w_ple_gate, i),
    )


def _run(x, p, layers, bias, expand_f, expand_b, final_g):
    bsz, s, d = x.shape
    m = bsz * s
    h = x.reshape(m, d)
    for i, lp in enumerate(layers):
        lam_init = 0.8 - 0.6 * math.exp(-0.3 * i)
        ps, pa, y_conv = proj_conv(h, lp["norm_mix_g"], lp["w_conv"], lp["w_ssd"], lp["w_att"],
                                   lp["conv_w"], lp["conv_b"], lp["conv_g"], lp["conv_beta"], s)
        ps = ps.reshape(bsz, s, SSD_SLAB)
        pa = pa.reshape(bsz, s, ATT_SLAB)
        y_ssd = ssd_mixer(ps, lp["ssd_cw"], lp["ssd_cb"], lp["ssd_dtb"], lp["ssd_alog"],
                          expand_f, expand_b, lp["ssd_dskip"], lp["ssd_ng"])
        y_att = diff_attention(pa, bias, lp["lam"], lp["subln_g"], lam_init)
        h = out_proj(y_conv, y_ssd.reshape(m, SSD_WIDTH),
                     y_att.reshape(m, ATT_WIDTH), h, lp["w_out"])
        h = mlp(h, lp["norm_mlp_g"], lp["w_up"], lp["w_down"])
        h = ple(h, p.reshape(len(layers), m, PLE_DIM), i, lp["norm_ple_g"], lp["w_gate"], lp["w_ple"],
                final_g, final=(i == len(layers) - 1))
    return h.reshape(bsz, s, d)


def kernel(x_prompt, x_sample, p_prompt, p_sample, norm_mix_g, w_in, conv_w, conv_b, conv_norm_g,
           conv_norm_b, ssd_conv_w, ssd_conv_b, ssd_dt_bias, ssd_a_log, ssd_d, ssd_norm_g,
           lambda_q1, lambda_k1, lambda_q2, lambda_k2, attn_subln_g, rel_bias, w_out, norm_mlp_g,
           w_up, w_down, norm_ple_g, w_ple, w_ple_gate, final_norm_g):
    layers = [_layer_params(i, norm_mix_g, w_in, conv_w, conv_b, conv_norm_g, conv_norm_b,
                            ssd_conv_w, ssd_conv_b, ssd_dt_bias, ssd_a_log, ssd_d, ssd_norm_g,
                            lambda_q1, lambda_k1, lambda_q2, lambda_k2, attn_subln_g, w_out,
                            norm_mlp_g, w_up, w_down, norm_ple_g, w_ple, w_ple_gate)
              for i in range(DEPTH)]
    bias = bias_tiles(_bucket_tiles(ATT_TILE), rel_bias)
    expand_f, expand_b = _expand_matrix(0), _expand_matrix(1)
    final_g = final_norm_g.reshape(1, -1)
    y_prompt = _run(x_prompt, p_prompt, layers, bias, expand_f, expand_b, final_g)
    y_sample = _run(x_sample, p_sample, layers, bias, expand_f, expand_b, final_g)
    return (y_prompt, y_sample)
```

```python
import functools
import math

import jax
import jax.numpy as jnp
import numpy as np
from jax import lax
from jax.experimental import pallas as pl
from jax.experimental.pallas import tpu as pltpu

F32 = jnp.float32
BF16 = jnp.bfloat16

D_MODEL = 2048
DEPTH = 2
PLE_DIM = 256
EPS = 1e-6
CONV_CH = 512
CONV_WIDTH = 31
CONV_PAD = (CONV_WIDTH - 1) // 2
SSD_HEAD_DIM = 64
SSD_HEADS = 12
SSD_WIDTH = SSD_HEADS * SSD_HEAD_DIM
SSD_GROUPS = 2
SSD_HPG = SSD_HEADS // SSD_GROUPS
SSD_STATE = 128
SSD_CONV_WIDTH = 5
SSD_CONV_PAD = (SSD_CONV_WIDTH - 1) // 2
SSD_CHUNK = 128
SSD_XBC = SSD_WIDTH + 2 * SSD_GROUPS * SSD_STATE
ATT_HEADS = 6
ATT_HEAD_DIM = 64
ATT_WIDTH = ATT_HEADS * 2 * ATT_HEAD_DIM
REL_BUCKETS = 32
REL_MAX_DIST = 128
D_FF = 4 * D_MODEL
LOG2E = math.log2(math.e)

LANES = 128
SUBLANES = 8
VMEM_LIMIT = 56 * 1024 * 1024

SSD_DT_COLS = LANES
SSD_SLAB = SSD_WIDTH + SSD_XBC + SSD_DT_COLS
CONV_SLAB = 2 * CONV_CH
ATT_SLAB = 3 * ATT_WIDTH
SSD_GW = SSD_HPG * SSD_HEAD_DIM

ATT_TILE = 512
ATT_QTILES = 2
N_BIAS_TILES = 5
ATT_ROWS = 32
CONV_TILE = 128
CONV_HALO = 16
CONV_ROWS = 32
SSD_HALO = SUBLANES
SSD_BATCH = 4
ROW_TILE = 512
DENSE_TILE = 1024
MLP_FF_TILE = 1024
CAST_BLOCK_BYTES = 8 * 1024 * 1024
SPLIT_ROWS = 256


def _cparams(sem):
    return pltpu.CompilerParams(dimension_semantics=sem, vmem_limit_bytes=VMEM_LIMIT)


def _rms(x, g):
    ms = jnp.mean(x * x, axis=-1, keepdims=True)
    return (x * lax.rsqrt(ms + EPS)) * g


def _silu(x):
    return x * jax.nn.sigmoid(x)


def _dot(a, b):
    return jnp.dot(a, b, preferred_element_type=F32)


def _split3(x):
    hi = x.astype(BF16)
    r1 = x - hi.astype(F32)
    mid = r1.astype(BF16)
    lo = (r1 - mid.astype(F32)).astype(BF16)
    return hi, mid, lo


def _select_dot(sel, x):
    sel = sel.astype(BF16)
    hi, mid, lo = _split3(x)
    return _dot(sel, hi) + (_dot(sel, mid) + _dot(sel, lo))


def _dot_select(x, sel):
    sel = sel.astype(BF16)
    hi = x.astype(BF16)
    mid = (x - hi.astype(F32)).astype(BF16)
    return _dot(hi, sel) + _dot(mid, sel)


def _dot_nt(a, b):
    return lax.dot_general(a, b, (((1,), (1,)), ((), ())), preferred_element_type=F32)


def _conv_tile(buf_ref, shift_ref, w, b_ref, g_ref, beta_ref, o_ref, row0):
    span = CONV_TILE + 2 * CONV_HALO - SUBLANES
    for b in range(1, SUBLANES):
        shift_ref[b - 1] = buf_ref[b:b + span, :]
    for r0 in range(0, CONV_TILE, CONV_ROWS):
        acc = jnp.broadcast_to(b_ref[...], (CONV_ROWS, CONV_CH))
        for k in range(CONV_WIDTH):
            start = r0 + CONV_HALO - CONV_PAD + k
            b, base = start % SUBLANES, start - start % SUBLANES
            window = (buf_ref[base:base + CONV_ROWS, :] if b == 0
                      else shift_ref[b - 1, base:base + CONV_ROWS, :])
            acc = acc + w[k:k + 1, :] * window
        mu = jnp.mean(acc, axis=-1, keepdims=True)
        xc = acc - mu
        var = jnp.mean(xc * xc, axis=-1, keepdims=True)
        y = xc * lax.rsqrt(var + EPS) * g_ref[...] + beta_ref[...]
        o_ref[row0 + r0:row0 + r0 + CONV_ROWS, :] = _silu(y).astype(o_ref.dtype)


def _proj_conv_kernel(x_ref, g_ref, wc_ref, ws_ref, wa_ref, cw_ref, cb_ref, cg_ref, cbeta_ref,
                      os_ref, oa_ref, oy_ref, ubuf_ref, unext_ref, buf_ref, shift_ref,
                      *, tiles_per_seq):
    tm, halo = ROW_TILE, CONV_HALO
    i = pl.program_id(0)

    @pl.when(i == 0)
    def _():
        ubuf_ref[...] = jnp.zeros_like(ubuf_ref)
        unext_ref[...] = jnp.zeros_like(unext_ref)

    ubuf_ref[0:halo, :] = ubuf_ref[tm:tm + halo, :]
    ubuf_ref[halo:halo + tm, :] = unext_ref[...]

    xn = _rms(x_ref[...], g_ref[...]).astype(BF16)
    pc = _dot(xn, wc_ref[...])
    u = pc[:, :CONV_CH] * jax.nn.sigmoid(pc[:, CONV_CH:])
    unext_ref[...] = u
    ubuf_ref[halo + tm:, :] = u[0:halo, :]
    os_ref[...] = _dot(xn, ws_ref[...])
    oa_ref[...] = _dot(xn, wa_ref[...]).astype(oa_ref.dtype)

    pos = lax.rem(i + tiles_per_seq - 1, tiles_per_seq)
    w = cw_ref[...]
    n_sub = tm // CONV_TILE
    for st in range(n_sub):
        base = st * CONV_TILE
        buf_ref[...] = ubuf_ref[base:base + CONV_TILE + 2 * halo, :]
        if st == 0:
            buf_ref[0:halo, :] = jnp.where(pos > 0, ubuf_ref[0:halo, :], 0.0)
        if st == n_sub - 1:
            buf_ref[halo + CONV_TILE:, :] = jnp.where(pos < tiles_per_seq - 1,
                                                      ubuf_ref[halo + tm:, :], 0.0)
        _conv_tile(buf_ref, shift_ref, w, cb_ref, cg_ref, cbeta_ref, oy_ref, base)


def proj_conv(x, g, w_conv, w_ssd, w_att, cw, cb, cg, cbeta, seq_len, tm=ROW_TILE):
    m, d = x.shape
    assert seq_len % tm == 0 and m % seq_len == 0
    n = m // tm
    resident = lambda w: pl.BlockSpec(w.shape, lambda i: (0, 0), pipeline_mode=pl.Buffered(1))
    cur = lambda cols: pl.BlockSpec((tm, cols), lambda i: (jnp.minimum(i, n - 1), 0))
    vec = lambda: pl.BlockSpec((1, CONV_CH), lambda i: (0, 0))
    return pl.pallas_call(
        functools.partial(_proj_conv_kernel, tiles_per_seq=seq_len // tm),
        grid=(n + 1,),
        in_specs=[cur(d), pl.BlockSpec((1, d), lambda i: (0, 0)),
                  resident(w_conv), resident(w_ssd), resident(w_att),
                  pl.BlockSpec((CONV_WIDTH, CONV_CH), lambda i: (0, 0)), vec(), vec(), vec()],
        out_specs=[cur(SSD_SLAB), cur(ATT_SLAB),
                   pl.BlockSpec((tm, CONV_CH), lambda i: (jnp.maximum(i - 1, 0), 0))],
        out_shape=[jax.ShapeDtypeStruct((m, SSD_SLAB), F32),
                   jax.ShapeDtypeStruct((m, ATT_SLAB), BF16),
                   jax.ShapeDtypeStruct((m, CONV_CH), BF16)],
        scratch_shapes=[pltpu.VMEM((tm + 2 * CONV_HALO, CONV_CH), F32),
                        pltpu.VMEM((tm, CONV_CH), F32),
                        pltpu.VMEM((CONV_TILE + 2 * CONV_HALO, CONV_CH), F32),
                        pltpu.VMEM((SUBLANES - 1, CONV_TILE + 2 * CONV_HALO - SUBLANES, CONV_CH),
                                   F32)],
        compiler_params=_cparams(("arbitrary",)),
        name="proj_conv",
    )(x, g, w_conv, w_ssd, w_att, cw, cb, cg, cbeta)


def _ssd_chunk(xbc, dt_raw, dtb, alog, expand, state_ref, bi, *, reverse):
    q = SSD_CHUNK
    xs = xbc[:, :SSD_WIDTH]
    bmat = xbc[:, SSD_WIDTH:SSD_WIDTH + SSD_GROUPS * SSD_STATE]
    cmat = xbc[:, SSD_WIDTH + SSD_GROUPS * SSD_STATE:]

    dt = jax.nn.softplus(dt_raw + dtb)
    dta = dt * (-jnp.exp(alog))
    row = lax.broadcasted_iota(jnp.int32, (q, q), 0)
    col = lax.broadcasted_iota(jnp.int32, (q, q), 1)
    keep = (row <= col) if reverse else (row >= col)
    a_cs = _select_dot(keep, dta)
    a_tot = jnp.sum(dta, axis=0, keepdims=True)
    ea = jnp.exp(a_cs)
    dec = jnp.exp(a_tot - a_cs)
    cdec = jnp.broadcast_to(jnp.exp(a_tot), (SUBLANES, LANES))
    ex = _dot_select(jnp.concatenate([dt, ea, dec, cdec], axis=0), expand)
    dtx, eax, decx, cdecx = ex[0:q], ex[q:2 * q], ex[2 * q:3 * q], ex[3 * q:3 * q + 1]
    xdt = xs * dtx
    xdt_b = xdt.astype(BF16)
    xd_b = (xdt * decx).astype(BF16)
    a_cs_t = a_cs.T
    lane = lax.broadcasted_iota(jnp.int32, (q, LANES), 1)
    lo_half = lane < SSD_HEAD_DIM
    lane0 = SSD_HEADS if reverse else 0

    y_groups = []
    for g in range(SSD_GROUPS):
        gs = slice(g * SSD_GW, (g + 1) * SSD_GW)
        bg = bmat[:, g * SSD_STATE:(g + 1) * SSD_STATE]
        cg_b = cmat[:, g * SSD_STATE:(g + 1) * SSD_STATE].astype(BF16)
        cb = _dot_nt(cg_b, bg.astype(BF16))
        prev = state_ref[bi, :, gs]
        y_off = _dot(cg_b, prev.astype(BF16)) * eax[:, gs]
        new_states = _dot(bg.T.astype(BF16), xd_b[:, gs])
        state_ref[bi, :, gs] = prev * cdecx[:, gs] + new_states
        pairs = []
        for j in range(SSD_HPG // 2):
            xpair = xdt_b[:, g * SSD_GW + j * LANES:g * SSD_GW + (j + 1) * LANES]
            ypair = None
            for half in range(2):
                ln = lane0 + g * SSD_HPG + 2 * j + half
                seg = a_cs[:, ln:ln + 1] - a_cs_t[ln:ln + 1, :]
                decay = jnp.exp(jnp.where(keep, seg, -jnp.inf))
                mixer = (cb * decay).astype(BF16)
                xh = jnp.where(lo_half if half == 0 else jnp.logical_not(lo_half), xpair,
                               jnp.zeros_like(xpair))
                part = _dot(mixer, xh)
                ypair = part if ypair is None else ypair + part
            pairs.append(ypair)
        y_groups.append(jnp.concatenate(pairs, axis=1) + y_off)
    return jnp.concatenate(y_groups, axis=1), xs


def _ssd_fwd_kernel(cur_ref, left_ref, right_ref, cw_ref, cb_ref, dtb_ref, alog_ref, exp_ref,
                    y_ref, xc_ref, state_ref, buf_ref):
    q = SSD_CHUNK
    c = pl.program_id(1)
    nc = pl.num_programs(1)

    @pl.when(c == 0)
    def _():
        state_ref[...] = jnp.zeros_like(state_ref)

    xbc_lo, xbc_hi = SSD_WIDTH, SSD_WIDTH + SSD_XBC
    cw = cw_ref[...]
    for bi in range(SSD_BATCH):
        blk = cur_ref[bi]
        buf_ref[bi, 0:SSD_HALO, :] = jnp.where(c > 0, left_ref[bi][:, xbc_lo:xbc_hi], 0.0)
        buf_ref[bi, SSD_HALO:SSD_HALO + q, :] = blk[:, xbc_lo:xbc_hi]
        buf_ref[bi, SSD_HALO + q:, :] = jnp.where(c < nc - 1, right_ref[bi][:, xbc_lo:xbc_hi], 0.0)
        acc = jnp.broadcast_to(cb_ref[...], (q, SSD_XBC))
        for k in range(SSD_CONV_WIDTH):
            start = SSD_HALO - SSD_CONV_PAD + k
            acc = acc + cw[k:k + 1, :] * buf_ref[bi, start:start + q, :]
        xbc = _silu(acc)
        xc_ref[bi] = xbc
        y, _ = _ssd_chunk(xbc, blk[:, xbc_hi:], dtb_ref[...], alog_ref[...], exp_ref[...],
                          state_ref, bi, reverse=False)
        y_ref[bi] = y


def _ssd_bwd_kernel(xc_ref, dt_ref, z_ref, yf_ref, dtb_ref, alog_ref, exp_ref, dskip_ref, ng_ref,
                    o_ref, state_ref):
    c = pl.program_id(1)

    @pl.when(c == 0)
    def _():
        state_ref[...] = jnp.zeros_like(state_ref)

    for bi in range(SSD_BATCH):
        y, xs = _ssd_chunk(xc_ref[bi], dt_ref[bi], dtb_ref[...], alog_ref[...], exp_ref[...],
                           state_ref, bi, reverse=True)
        y = y + yf_ref[bi] + dskip_ref[...] * xs
        o_ref[bi] = _rms(y * _silu(z_ref[bi]), ng_ref[...]).astype(o_ref.dtype)


def ssd_mixer(ps, cw, cb, dtb, alog, expand_f, expand_b, dskip, ng):
    bsz, s, _ = ps.shape
    q, nb = SSD_CHUNK, SSD_BATCH
    assert bsz % nb == 0 and s % q == 0
    nc = s // q
    hb = q // SSD_HALO
    last_halo = s // SSD_HALO - 1
    full = lambda shape: pl.BlockSpec(shape, lambda i, c: (0,) * len(shape))
    params = [full((1, LANES)), full((1, LANES)), full((LANES, SSD_WIDTH))]
    state = pltpu.VMEM((nb, SSD_STATE, SSD_WIDTH), F32)
    fwd = lambda width, col=0: pl.BlockSpec((nb, q, width), lambda i, c: (i, c, col))
    y_f, xc = pl.pallas_call(
        _ssd_fwd_kernel,
        grid=(bsz // nb, nc),
        in_specs=[fwd(SSD_SLAB),
                  pl.BlockSpec((nb, SSD_HALO, SSD_SLAB),
                               lambda i, c: (i, jnp.maximum(c * hb - 1, 0), 0)),
                  pl.BlockSpec((nb, SSD_HALO, SSD_SLAB),
                               lambda i, c: (i, jnp.minimum((c + 1) * hb, last_halo), 0)),
                  full((SSD_CONV_WIDTH, SSD_XBC)), full((1, SSD_XBC))] + params,
        out_specs=[fwd(SSD_WIDTH), fwd(SSD_XBC)],
        out_shape=[jax.ShapeDtypeStruct((bsz, s, SSD_WIDTH), F32),
                   jax.ShapeDtypeStruct((bsz, s, SSD_XBC), F32)],
        scratch_shapes=[state, pltpu.VMEM((nb, q + 2 * SSD_HALO, SSD_XBC), F32)],
        compiler_params=_cparams(("parallel", "arbitrary")),
        name="ssd_fwd",
    )(ps, ps, ps, cw, cb, dtb, alog, expand_f)
    bwd = lambda width, col=0: pl.BlockSpec((nb, q, width), lambda i, c: (i, nc - 1 - c, col))
    dt_col = (SSD_WIDTH + SSD_XBC) // SSD_DT_COLS
    return pl.pallas_call(
        _ssd_bwd_kernel,
        grid=(bsz // nb, nc),
        in_specs=[bwd(SSD_XBC), bwd(SSD_DT_COLS, dt_col), bwd(SSD_WIDTH), bwd(SSD_WIDTH)]
                 + params + [full((1, SSD_WIDTH)), full((1, SSD_WIDTH))],
        out_specs=bwd(SSD_WIDTH),
        out_shape=jax.ShapeDtypeStruct((bsz, s, SSD_WIDTH), BF16),
        scratch_shapes=[state],
        compiler_params=_cparams(("parallel", "arbitrary")),
        name="ssd_bwd",
    )(xc, ps, ps, y_f, dtb, alog, expand_b, dskip, ng)


def _bias_tiles_kernel(bucket_ref, table_ref, o_ref):
    h = pl.program_id(0)
    bucket = bucket_ref[0]
    out = jnp.zeros(bucket.shape, F32)
    for b in range(REL_BUCKETS):
        out = jnp.where(bucket == b, table_ref[b, h] * LOG2E, out)
    o_ref[0, 0] = out


def bias_tiles(buckets, rel_bias):
    nt, t, _ = buckets.shape
    return pl.pallas_call(
        _bias_tiles_kernel,
        grid=(ATT_HEADS, nt),
        in_specs=[pl.BlockSpec((1, t, t), lambda h, d: (d, 0, 0)),
                  pl.BlockSpec(memory_space=pltpu.SMEM)],
        out_specs=pl.BlockSpec((1, 1, t, t), lambda h, d: (h, d, 0, 0)),
        out_shape=jax.ShapeDtypeStruct((ATT_HEADS, nt, t, t), F32),
        compiler_params=_cparams(("parallel", "parallel")),
        name="bias_tiles",
    )(buckets, rel_bias)


def _rel_bucket(rel):
    nb = REL_BUCKETS // 2
    max_exact = nb // 2
    ret = jnp.where(rel > 0, nb, 0)
    n = jnp.abs(rel)
    nf = jnp.maximum(n, 1).astype(jnp.float32)
    large = max_exact + (jnp.log(nf / max_exact) / math.log(REL_MAX_DIST / max_exact)
                         * (nb - max_exact)).astype(jnp.int32)
    large = jnp.minimum(large, nb - 1)
    return ret + jnp.where(n < max_exact, n, large)


def _bucket_tiles(t):
    i = jnp.arange(t, dtype=jnp.int32)
    half = N_BIAS_TILES // 2
    offs = (jnp.arange(N_BIAS_TILES, dtype=jnp.int32) - half) * t
    rel = offs[:, None, None] + i[None, None, :] - i[None, :, None]
    return _rel_bucket(rel).astype(jnp.int32)


def _attn_kernel(q_ref, k_ref, v_ref, bias_ref, lam_ref, g_ref, o_ref,
                 qz_ref, s_ref, p_ref, m_ref, l_ref, alpha_ref, acc_ref, *, lam_init, nk):
    t = ATT_TILE
    tq = ATT_QTILES * t
    qi = pl.program_id(2)
    qb = q_ref[0]
    lane = lax.broadcasted_iota(jnp.int32, (tq, 2 * ATT_HEAD_DIM), 1)
    qz_ref[0] = jnp.where(lane < ATT_HEAD_DIM, qb, jnp.zeros_like(qb))
    qz_ref[1] = jnp.where(lane >= ATT_HEAD_DIM, qb, jnp.zeros_like(qb))
    m_ref[...] = jnp.full(m_ref.shape, -jnp.inf, F32)
    l_ref[...] = jnp.zeros_like(l_ref)
    acc_ref[...] = jnp.zeros_like(acc_ref)
    half = N_BIAS_TILES // 2
    rb = ATT_ROWS

    def key_rows(kj):
        start = kj * t
        return pl.ds(start if isinstance(start, int) else pl.multiple_of(start, t), t)

    def apply_pv(slot, kj):
        vb = v_ref[0, key_rows(kj), :]
        for mp in range(2):
            acc_ref[mp] = alpha_ref[slot, mp] * acc_ref[mp] + _dot(p_ref[slot, mp], vb)

    def logits(slot, kj):
        kb = k_ref[0, key_rows(kj), :]
        for mp in range(2):
            s_ref[slot, mp] = _dot_nt(qz_ref[mp], kb)

    def softmax(slot, kj):
        tiles = [jnp.clip(kj - (ATT_QTILES * qi + u), -half, half) + half
                 for u in range(ATT_QTILES)]
        for mp in range(2):
            for r0 in range(0, tq, rb):
                rows = slice(r0, r0 + rb)
                bias_rows = slice(r0 % t, r0 % t + rb)
                s = s_ref[slot, mp, rows, :] + bias_ref[0, tiles[r0 // t], bias_rows, :]
                m_old = m_ref[mp, rows, :]
                m_new = jnp.maximum(m_old, jnp.max(s, axis=-1, keepdims=True))
                alpha = jnp.exp2(m_old - m_new)
                p = jnp.exp2(s - jnp.concatenate([m_new] * (t // LANES), axis=1))
                l_ref[mp, rows, :] = (alpha * l_ref[mp, rows, :]
                                      + jnp.sum(p, axis=-1, keepdims=True))
                m_ref[mp, rows, :] = m_new
                alpha_ref[slot, mp, rows, :] = alpha
                p_ref[slot, mp, rows, :] = p.astype(BF16)

    def pair(kk, carry, first=False):
        for slot in range(2):
            kj = 2 * kk + slot
            logits(slot, kj)
            if not (first and slot == 0):
                apply_pv(1 - slot, kj - 1)
            softmax(slot, kj)
        return carry

    pair(0, 0, first=True)
    lax.fori_loop(1, nk // 2, pair, 0)
    apply_pv(1, nk - 1)

    lp = lam_ref[...]
    lam = (jnp.exp(jnp.sum(lp[0:1] * lp[1:2], axis=-1, keepdims=True))
           - jnp.exp(jnp.sum(lp[2:3] * lp[3:4], axis=-1, keepdims=True)) + lam_init)
    o = acc_ref[0] / l_ref[0] - lam * (acc_ref[1] / l_ref[1])
    o_ref[0] = (_rms(o, g_ref[...]) * (1.0 - lam_init)).astype(o_ref.dtype)


def diff_attention(qkv, bias, lam_params, subln_g, lam_init):
    bsz, s, _ = qkv.shape
    t = ATT_TILE
    tq = ATT_QTILES * t
    nk = s // t
    hw = 2 * ATT_HEAD_DIM
    assert nk % 2 == 0 and s % tq == 0
    stat = pltpu.VMEM((2, tq, LANES), F32)
    return pl.pallas_call(
        functools.partial(_attn_kernel, lam_init=lam_init, nk=nk),
        grid=(bsz, ATT_HEADS, s // tq),
        in_specs=[
            pl.BlockSpec((1, tq, hw), lambda b, h, i: (b, i, h)),
            pl.BlockSpec((1, s, hw), lambda b, h, i: (b, 0, ATT_HEADS + h)),
            pl.BlockSpec((1, s, hw), lambda b, h, i: (b, 0, 2 * ATT_HEADS + h)),
            pl.BlockSpec((1, N_BIAS_TILES, t, t), lambda b, h, i: (h, 0, 0, 0)),
            pl.BlockSpec((4, ATT_HEAD_DIM), lambda b, h, i: (0, 0)),
            pl.BlockSpec((1, hw), lambda b, h, i: (0, 0)),
        ],
        out_specs=pl.BlockSpec((1, tq, hw), lambda b, h, i: (b, i, h)),
        out_shape=jax.ShapeDtypeStruct((bsz, s, ATT_WIDTH), BF16),
        scratch_shapes=[pltpu.VMEM((2, tq, hw), BF16),
                        pltpu.VMEM((2, 2, tq, t), F32),
                        pltpu.VMEM((2, 2, tq, t), BF16),
                        stat, stat,
                        pltpu.VMEM((2, 2, tq, LANES), F32),
                        pltpu.VMEM((2, tq, hw), F32)],
        compiler_params=_cparams(("parallel", "parallel", "parallel")),
        name="diff_attention",
    )(qkv, qkv, qkv, bias, lam_params, subln_g)


def _mlp_kernel(yc_ref, ys_ref, ya_ref, h_ref, wo_ref, g_ref, wu_ref, wd_ref, o_ref, xn_ref):
    f = pl.program_id(1)

    @pl.when(f == 0)
    def _():
        c0, c1 = CONV_CH, CONV_CH + SSD_WIDTH
        acc = _dot(yc_ref[...], wo_ref[0:c0, :])
        acc = acc + _dot(ys_ref[...], wo_ref[c0:c1, :])
        acc = acc + _dot(ya_ref[...], wo_ref[c1:, :])
        h = h_ref[...] + acc
        xn_ref[...] = _rms(h, g_ref[...]).astype(BF16)
        o_ref[...] = h

    hid = jnp.square(jnp.maximum(_dot(xn_ref[...], wu_ref[...]), 0.0)).astype(BF16)
    o_ref[...] += _dot(hid, wd_ref[...])


def out_proj_mlp(yc, ys, ya, h, wo, g, wu, wd, tm=ROW_TILE, tf=MLP_FF_TILE):
    m, d = h.shape
    ff = wu.shape[1]
    row = lambda n: pl.BlockSpec((tm, n), lambda i, f: (i, 0))
    return pl.pallas_call(
        _mlp_kernel,
        grid=(m // tm, ff // tf),
        in_specs=[row(CONV_CH), row(SSD_WIDTH), row(ATT_WIDTH), row(d),
                  pl.BlockSpec(wo.shape, lambda i, f: (0, 0), pipeline_mode=pl.Buffered(1)),
                  pl.BlockSpec((1, d), lambda i, f: (0, 0)),
                  pl.BlockSpec((d, tf), lambda i, f: (0, f)),
                  pl.BlockSpec((tf, d), lambda i, f: (f, 0))],
        out_specs=row(d),
        out_shape=jax.ShapeDtypeStruct((m, d), F32),
        scratch_shapes=[pltpu.VMEM((tm, d), BF16)],
        compiler_params=_cparams(("parallel", "arbitrary")),
        name="out_proj_mlp",
    )(yc, ys, ya, h, wo, g, wu, wd)


def _ple_kernel(h_ref, p_ref, g_ref, wg_ref, wp_ref, gf_ref, o_ref, *, final):
    h = h_ref[...]
    gate = jax.nn.sigmoid(_dot(_rms(h, g_ref[...]).astype(BF16), wg_ref[...]))
    out = h + _dot(p_ref[0].astype(BF16), wp_ref[...]) * gate
    if final:
        out = _rms(out, gf_ref[...])
    o_ref[...] = out


def ple(h, p, layer, g, wg, wp, gf, final, tm=DENSE_TILE):
    m, d = h.shape
    vec = pl.BlockSpec((1, d), lambda i: (0, 0))
    return pl.pallas_call(
        functools.partial(_ple_kernel, final=final),
        grid=(m // tm,),
        in_specs=[pl.BlockSpec((tm, d), lambda i: (i, 0)),
                  pl.BlockSpec((1, tm, PLE_DIM), lambda i: (layer, i, 0)),
                  vec,
                  pl.BlockSpec((d, d), lambda i: (0, 0), pipeline_mode=pl.Buffered(1)),
                  pl.BlockSpec((PLE_DIM, d), lambda i: (0, 0), pipeline_mode=pl.Buffered(1)),
                  vec],
        out_specs=pl.BlockSpec((tm, d), lambda i: (i, 0)),
        out_shape=jax.ShapeDtypeStruct((m, d), F32),
        compiler_params=_cparams(("parallel",)),
        name="ple_final" if final else "ple",
    )(h, p, g, wg, wp, gf)


def _split_w_in_kernel(w_ref, oc_ref, os_ref, oa_ref):
    c_conv = 2 * CONV_CH
    c_att = c_conv + SSD_WIDTH + SSD_XBC + 2 * SSD_HEADS
    w = w_ref[0]
    oc_ref[...] = w[:, :c_conv].astype(BF16)
    body = SSD_SLAB - SSD_DT_COLS
    tail = w[:, c_conv + body:c_conv + body + SSD_DT_COLS]
    lane = lax.broadcasted_iota(jnp.int32, tail.shape, 1)
    tail = jnp.where(lane < 2 * SSD_HEADS, tail, 0.0)
    os_ref[...] = jnp.concatenate([w[:, c_conv:c_conv + body], tail], axis=1).astype(BF16)
    q = w[:, c_att:c_att + ATT_WIDTH] * (LOG2E * ATT_HEAD_DIM ** -0.5)
    oa_ref[...] = jnp.concatenate([q, w[:, c_att + ATT_WIDTH:]], axis=1).astype(BF16)


def split_w_in(w_in, layer):
    _, r, c = w_in.shape
    tr = SPLIT_ROWS
    assert r % tr == 0
    out = lambda n: pl.BlockSpec((tr, n), lambda j: (j, 0))
    return pl.pallas_call(
        _split_w_in_kernel,
        grid=(r // tr,),
        in_specs=[pl.BlockSpec((1, tr, c), lambda j: (layer, j, 0))],
        out_specs=[out(CONV_SLAB), out(SSD_SLAB), out(ATT_SLAB)],
        out_shape=[jax.ShapeDtypeStruct((r, CONV_SLAB), BF16),
                   jax.ShapeDtypeStruct((r, SSD_SLAB), BF16),
                   jax.ShapeDtypeStruct((r, ATT_SLAB), BF16)],
        compiler_params=_cparams(("parallel",)),
        name="split_w_in",
    )(w_in)


def _cast_kernel(w_ref, o_ref):
    o_ref[...] = w_ref[0].astype(o_ref.dtype)


def cast_layer(w, layer):
    _, r, c = w.shape
    tr = min(r, CAST_BLOCK_BYTES // (4 * c))
    assert r % tr == 0 and tr % SUBLANES == 0
    return pl.pallas_call(
        _cast_kernel,
        grid=(r // tr,),
        in_specs=[pl.BlockSpec((1, tr, c), lambda j: (layer, j, 0))],
        out_specs=pl.BlockSpec((tr, c), lambda j: (j, 0)),
        out_shape=jax.ShapeDtypeStruct((r, c), BF16),
        compiler_params=_cparams(("parallel",)),
        name="cast_layer",
    )(w)


def _expand_matrix(direction):
    e = np.zeros((LANES, SSD_WIDTH), np.float32)
    for hd in range(SSD_HEADS):
        e[direction * SSD_HEADS + hd, hd * SSD_HEAD_DIM:(hd + 1) * SSD_HEAD_DIM] = 1.0
    return jnp.asarray(e)


def _pad_lanes(v):
    return jnp.pad(v.reshape(1, -1), ((0, 0), (0, LANES - v.size)))


def _layer_params(i, norm_mix_g, w_in, conv_w, conv_b, conv_norm_g, conv_norm_b, ssd_conv_w,
                  ssd_conv_b, ssd_dt_bias, ssd_a_log, ssd_d, ssd_norm_g, lambda_q1, lambda_k1,
                  lambda_q2, lambda_k2, attn_subln_g, w_out, norm_mlp_g, w_up, w_down,
                  norm_ple_g, w_ple, w_ple_gate):
    row = lambda v: v.reshape(1, -1)
    w_conv, w_ssd, w_att = split_w_in(w_in, i)
    return dict(
        norm_mix_g=row(norm_mix_g[i]),
        w_conv=w_conv, w_ssd=w_ssd, w_att=w_att,
        conv_w=conv_w[i], conv_b=row(conv_b[i]), conv_g=row(conv_norm_g[i]),
        conv_beta=row(conv_norm_b[i]),
        ssd_cw=ssd_conv_w[i], ssd_cb=row(ssd_conv_b[i]),
        ssd_dtb=_pad_lanes(ssd_dt_bias[i]), ssd_alog=_pad_lanes(ssd_a_log[i]),
        ssd_dskip=row(jnp.repeat(ssd_d[i], SSD_HEAD_DIM)), ssd_ng=row(ssd_norm_g[i]),
        lam=jnp.stack([lambda_q1[i], lambda_k1[i], lambda_q2[i], lambda_k2[i]]),
        subln_g=row(attn_subln_g[i]),
        w_out=cast_layer(w_out, i),
        norm_mlp_g=row(norm_mlp_g[i]),
        w_up=cast_layer(w_up, i), w_down=cast_layer(w_down, i),
        norm_ple_g=row(norm_ple_g[i]),
        w_ple=cast_layer(w_ple, i), w_gate=cast_layer(w_ple_gate, i),
    )


def _run(x, p, layers, bias, expand_f, expand_b, final_g):
    bsz, s, d = x.shape
    m = bsz * s
    h = x.reshape(m, d)
    for i, lp in enumerate(layers):
        lam_init = 0.8 - 0.6 * math.exp(-0.3 * i)
        ps, pa, y_conv = proj_conv(h, lp["norm_mix_g"], lp["w_conv"], lp["w_ssd"], lp["w_att"],
                                   lp["conv_w"], lp["conv_b"], lp["conv_g"], lp["conv_beta"], s)
        ps = ps.reshape(bsz, s, SSD_SLAB)
        pa = pa.reshape(bsz, s, ATT_SLAB)
        y_ssd = ssd_mixer(ps, lp["ssd_cw"], lp["ssd_cb"], lp["ssd_dtb"], lp["ssd_alog"],
                          expand_f, expand_b, lp["ssd_dskip"], lp["ssd_ng"])
        y_att = diff_attention(pa, bias, lp["lam"], lp["subln_g"], lam_init)
        h = out_proj_mlp(y_conv, y_ssd.reshape(m, SSD_WIDTH), y_att.reshape(m, ATT_WIDTH), h,
                         lp["w_out"], lp["norm_mlp_g"], lp["w_up"], lp["w_down"])
        h = ple(h, p.reshape(len(layers), m, PLE_DIM), i, lp["norm_ple_g"], lp["w_gate"], lp["w_ple"],
                final_g, final=(i == len(layers) - 1))
    return h.reshape(bsz, s, d)


def kernel(x_prompt, x_sample, p_prompt, p_sample, norm_mix_g, w_in, conv_w, conv_b, conv_norm_g,
           conv_norm_b, ssd_conv_w, ssd_conv_b, ssd_dt_bias, ssd_a_log, ssd_d, ssd_norm_g,
           lambda_q1, lambda_k1, lambda_q2, lambda_k2, attn_subln_g, rel_bias, w_out, norm_mlp_g,
           w_up, w_down, norm_ple_g, w_ple, w_ple_gate, final_norm_g):
    layers = [_layer_params(i, norm_mix_g, w_in, conv_w, conv_b, conv_norm_g, conv_norm_b,
                            ssd_conv_w, ssd_conv_b, ssd_dt_bias, ssd_a_log, ssd_d, ssd_norm_g,
                            lambda_q1, lambda_k1, lambda_q2, lambda_k2, attn_subln_g, w_out,
                            norm_mlp_g, w_up, w_down, norm_ple_g, w_ple, w_ple_gate)
              for i in range(DEPTH)]
    bias = bias_tiles(_bucket_tiles(ATT_TILE), rel_bias)
    expand_f, expand_b = _expand_matrix(0), _expand_matrix(1)
    final_g = final_norm_g.reshape(1, -1)
    y_prompt = _run(x_prompt, p_prompt, layers, bias, expand_f, expand_b, final_g)
    y_sample = _run(x_sample, p_sample, layers, bias, expand_f, expand_b, final_g)
    return (y_prompt, y_sample)
```
